```python
import jax, jax.numpy as jnp
from jax import lax
import numpy as np

D_MODEL = 1024
BATCH = 16
SEQ = 4096
DEPTH = 2

N_HEADS = 8
HEAD_DIM = 64
N_KV_HEADS = 2
GROUP = N_HEADS // N_KV_HEADS
ATTN_WIDTH = N_HEADS * HEAD_DIM
KV_WIDTH = 2 * N_KV_HEADS * HEAD_DIM
N_NSA_BRANCHES = 3
CMP_BLOCK = 32
CMP_STRIDE = 16
CMP_HIDDEN = 64
SEL_BLOCK = 64
SEL_TOP_N = 16
WINDOW = 512
Q_BLOCK = 128
POOL_GROUPS = 4
POOL_GROUP_DIM = 128
POOL_WIDTH = POOL_GROUPS * POOL_GROUP_DIM
POOL_WINDOWS = (2, 4, 8, 16)
N_MIXERS = 2
D_FF = 4 * D_MODEL
IN_WIDTH = ATTN_WIDTH + N_NSA_BRANCHES * KV_WIDTH + N_NSA_BRANCHES * N_HEADS + POOL_WIDTH + N_MIXERS * D_MODEL
EPS = 1e-6
NEG = -1e30
FORCE = 1e6

kernel_name = "nsa_pool_gated_hybrid"


def rmsnorm(x, g):
    xf = x.astype(jnp.float32)
    y = xf * lax.rsqrt(jnp.mean(xf * xf, axis=-1, keepdims=True) + EPS)
    return (y * g.astype(jnp.float32)).astype(x.dtype)


def alibi_slopes():
    s = np.array([2.0 ** (-8.0 * (h + 1) / N_HEADS) for h in range(N_HEADS)], dtype=np.float32)
    return jnp.asarray(s.reshape(N_KV_HEADS, GROUP))


def compress_kv(kv, pe, w1, w2):
    S = kv.shape[1]
    n_cmp = (S - CMP_BLOCK) // CMP_STRIDE + 1
    idx = jnp.arange(n_cmp)[:, None] * CMP_STRIDE + jnp.arange(CMP_BLOCK)[None, :]
    blocks = kv[:, idx] + pe[None, None, :, None, :]
    h = jax.nn.gelu(jnp.einsum('bclgd,ldh->bcgh', blocks, w1))
    return jnp.einsum('bcgh,he->bcge', h, w2)


def nsa_attention(q, k_cmp, v_cmp, k_slc, v_slc, k_win, v_win, gates):
    B, S = q.shape[0], q.shape[1]
    nb = S // Q_BLOCK
    n_sel = S // SEL_BLOCK
    top_n = min(SEL_TOP_N, n_sel)
    n_cmp = k_cmp.shape[1]
    slopes = alibi_slopes()
    scale = HEAD_DIM ** -0.5

    cmp_start = jnp.arange(n_cmp) * CMP_STRIDE
    cmp_end = cmp_start + CMP_BLOCK - 1
    sel_start = jnp.arange(n_sel) * SEL_BLOCK
    sel_end = sel_start + SEL_BLOCK - 1
    overlap = ((cmp_start[:, None] <= sel_end[None, :]) &
               (cmp_end[:, None] >= sel_start[None, :])).astype(jnp.float32)

    k_blocks = k_slc.reshape(B, n_sel, SEL_BLOCK, N_KV_HEADS, HEAD_DIM).transpose(0, 3, 1, 2, 4)
    v_blocks = v_slc.reshape(B, n_sel, SEL_BLOCK, N_KV_HEADS, HEAD_DIM).transpose(0, 3, 1, 2, 4)
    k_pad = jnp.pad(k_win, ((0, 0), (WINDOW, 0), (0, 0), (0, 0)))
    v_pad = jnp.pad(v_win, ((0, 0), (WINDOW, 0), (0, 0), (0, 0)))
    bi = jnp.arange(B)[:, None, None, None]
    gi = jnp.arange(N_KV_HEADS)[None, :, None, None]
    j = jnp.arange(n_sel)

    q_blocks = q.reshape(B, nb, Q_BLOCK, N_KV_HEADS, GROUP, HEAD_DIM).transpose(1, 0, 2, 3, 4, 5)
    g_blocks = gates.reshape(B, nb, Q_BLOCK, N_KV_HEADS, GROUP, N_NSA_BRANCHES).transpose(1, 0, 2, 3, 4, 5)

    def one_block(args):
        i, q_blk, g_blk = args
        q0 = i * Q_BLOCK
        t = q0 + jnp.arange(Q_BLOCK)
        qs = q_blk * scale

        dist_c = (t[:, None] - cmp_end[None, :]).astype(jnp.float32)
        mask_c = dist_c >= 0
        s_c = jnp.einsum('bqgrd,bcgd->bgrqc', qs, k_cmp).astype(jnp.float32)
        s_c = s_c - slopes[None, :, :, None, None] * dist_c
        p_c = jax.nn.softmax(jnp.where(mask_c, s_c, NEG), axis=-1) * mask_c
        o_c = jnp.einsum('bgrqc,bcgd->bqgrd', p_c.astype(v_cmp.dtype), v_cmp)

        imp = jnp.einsum('bgrqc,cj->bgqj', p_c, overlap)
        cur = t // SEL_BLOCK
        future = j[None, :] > cur[:, None]
        forced = (j[None, :] == 0) | (j[None, :] == cur[:, None]) | (j[None, :] == cur[:, None] - 1)
        imp = jnp.where(future, NEG, jnp.where(forced, FORCE, imp))
        top_val, top_idx = lax.top_k(imp, top_n)
        valid = top_val > NEG / 2
        k_g = k_blocks[bi, gi, top_idx]
        v_g = v_blocks[bi, gi, top_idx]
        pos_s = top_idx[..., None] * SEL_BLOCK + jnp.arange(SEL_BLOCK)
        dist_s = (t[None, None, :, None, None] - pos_s).astype(jnp.float32)
        mask_s = valid[..., None] & (dist_s >= 0)
        s_s = jnp.einsum('bqgrd,bgqnld->bgrqnl', qs, k_g).astype(jnp.float32)
        s_s = s_s - slopes[None, :, :, None, None, None] * dist_s[:, :, None]
        s_s = jnp.where(mask_s[:, :, None], s_s, NEG).reshape(B, N_KV_HEADS, GROUP, Q_BLOCK, top_n * SEL_BLOCK)
        p_s = jax.nn.softmax(s_s, axis=-1).reshape(B, N_KV_HEADS, GROUP, Q_BLOCK, top_n, SEL_BLOCK)
        o_s = jnp.einsum('bgrqnl,bgqnld->bqgrd', p_s.astype(v_g.dtype), v_g)

        k_w = lax.dynamic_slice_in_dim(k_pad, q0, WINDOW + Q_BLOCK, axis=1)
        v_w = lax.dynamic_slice_in_dim(v_pad, q0, WINDOW + Q_BLOCK, axis=1)
        pos_w = q0 - WINDOW + jnp.arange(WINDOW + Q_BLOCK)
        dist_wi = t[:, None] - pos_w[None, :]
        mask_w = (dist_wi >= 0) & (dist_wi < WINDOW) & (pos_w >= 0)[None, :]
        s_w = jnp.einsum('bqgrd,bkgd->bgrqk', qs, k_w).astype(jnp.float32)
        s_w = s_w - slopes[None, :, :, None, None] * dist_wi.astype(jnp.float32)
        p_w = jax.nn.softmax(jnp.where(mask_w, s_w, NEG), axis=-1)
        o_w = jnp.einsum('bgrqk,bkgd->bqgrd', p_w.astype(v_w.dtype), v_w)

        o = g_blk[..., 0:1] * o_c + g_blk[..., 1:2] * o_s + g_blk[..., 2:3] * o_w
        return o.reshape(B, Q_BLOCK, ATTN_WIDTH)

    out = lax.map(one_block, (jnp.arange(nb), q_blocks, g_blocks))
    return out.transpose(1, 0, 2, 3).reshape(B, S, ATTN_WIDTH)


def pool_mixer(u, w_pool, pool_scale):
    B, S, _ = u.shape
    uf = u.astype(jnp.float32)
    c = jnp.pad(jnp.cumsum(uf, axis=1), ((0, 0), (1, 0), (0, 0)))
    t = jnp.arange(S)
    outs = []
    for gidx, w in enumerate(POOL_WINDOWS):
        sl = slice(gidx * POOL_GROUP_DIM, (gidx + 1) * POOL_GROUP_DIM)
        lo = jnp.maximum(t + 1 - w, 0)
        cnt = (t + 1 - lo).astype(jnp.float32)
        mean = (c[:, 1:, sl] - c[:, lo, sl]) / cnt[None, :, None]
        outs.append(mean - uf[..., sl])
    p = jnp.stack(outs, axis=2).astype(u.dtype)
    y = jnp.einsum('bsgc,gcd->bsgd', p, w_pool).reshape(B, S, POOL_WIDTH)
    return y * pool_scale


def setup_inputs(seed: int = 0) -> dict:
    key = jax.random.key(seed)
    ks = jax.random.split(key, 16)

    def nrm(k, shape, fan_in):
        return jax.random.normal(k, shape, jnp.float32) * (fan_in ** -0.5)

    def gain(k, shape):
        return 1.0 + 0.02 * jax.random.normal(k, shape, jnp.float32)

    return {
        "x": jax.random.normal(ks[0], (BATCH, SEQ, D_MODEL), jnp.float32),
        "norm1_g": gain(ks[1], (DEPTH, D_MODEL)),
        "w_in": nrm(ks[2], (DEPTH, D_MODEL, IN_WIDTH), D_MODEL),
        "cmp_pe": 0.02 * jax.random.normal(ks[3], (DEPTH, 2, CMP_BLOCK, HEAD_DIM), jnp.float32),
        "cmp_w1": nrm(ks[4], (DEPTH, 2, CMP_BLOCK, HEAD_DIM, CMP_HIDDEN), CMP_BLOCK * HEAD_DIM),
        "cmp_w2": nrm(ks[5], (DEPTH, 2, CMP_HIDDEN, HEAD_DIM), CMP_HIDDEN),
        "w_attn_proj": nrm(ks[6], (DEPTH, ATTN_WIDTH, D_MODEL), ATTN_WIDTH),
        "w_pool": nrm(ks[7], (DEPTH, POOL_GROUPS, POOL_GROUP_DIM, POOL_GROUP_DIM), POOL_GROUP_DIM),
        "pool_scale": gain(ks[8], (DEPTH, POOL_WIDTH)),
        "w_pool_proj": nrm(ks[9], (DEPTH, POOL_WIDTH, D_MODEL), POOL_WIDTH),
        "w_out": nrm(ks[10], (DEPTH, D_MODEL, D_MODEL), D_MODEL),
        "norm2_g": gain(ks[11], (DEPTH, D_MODEL)),
        "w_ff1": nrm(ks[12], (DEPTH, D_MODEL, D_FF), D_MODEL),
        "w_ff2": nrm(ks[13], (DEPTH, D_FF, D_MODEL), D_FF),
        "final_g": gain(ks[14], (D_MODEL,)),
    }


def reference(x, norm1_g, w_in, cmp_pe, cmp_w1, cmp_w2, w_attn_proj, w_pool, pool_scale,
              w_pool_proj, w_out, norm2_g, w_ff1, w_ff2, final_g):
    B, S, _ = x.shape
    hk = N_KV_HEADS * HEAD_DIM
    o_q = 0
    o_kv = o_q + ATTN_WIDTH
    o_gate = o_kv + N_NSA_BRANCHES * KV_WIDTH
    o_pool = o_gate + N_NSA_BRANCHES * N_HEADS
    o_merge = o_pool + POOL_WIDTH
    for l in range(DEPTH):
        h = rmsnorm(x, norm1_g[l])
        proj = h @ w_in[l]
        q = proj[..., o_q:o_kv].reshape(B, S, N_KV_HEADS, GROUP, HEAD_DIM)
        kvs = [proj[..., o_kv + bidx * KV_WIDTH: o_kv + (bidx + 1) * KV_WIDTH] for bidx in range(N_NSA_BRANCHES)]
        k_c = kvs[0][..., :hk].reshape(B, S, N_KV_HEADS, HEAD_DIM)
        v_c = kvs[0][..., hk:].reshape(B, S, N_KV_HEADS, HEAD_DIM)
        k_s = kvs[1][..., :hk].reshape(B, S, N_KV_HEADS, HEAD_DIM)
        v_s = kvs[1][..., hk:].reshape(B, S, N_KV_HEADS, HEAD_DIM)
        k_w = kvs[2][..., :hk].reshape(B, S, N_KV_HEADS, HEAD_DIM)
        v_w = kvs[2][..., hk:].reshape(B, S, N_KV_HEADS, HEAD_DIM)
        nsa_gates = jax.nn.sigmoid(proj[..., o_gate:o_pool]).reshape(B, S, N_KV_HEADS, GROUP, N_NSA_BRANCHES)
        u_pool = proj[..., o_pool:o_merge]
        merge = jax.nn.sigmoid(proj[..., o_merge:]).reshape(B, S, N_MIXERS, D_MODEL)

        k_cmp = compress_kv(k_c, cmp_pe[l, 0], cmp_w1[l, 0], cmp_w2[l, 0])
        v_cmp = compress_kv(v_c, cmp_pe[l, 1], cmp_w1[l, 1], cmp_w2[l, 1])
        a = nsa_attention(q, k_cmp, v_cmp, k_s, v_s, k_w, v_w, nsa_gates) @ w_attn_proj[l]
        b = pool_mixer(u_pool, w_pool[l], pool_scale[l]) @ w_pool_proj[l]
        x = x + (merge[..., 0, :] * a + merge[..., 1, :] * b) @ w_out[l]

        h = rmsnorm(x, norm2_g[l])
        x = x + jnp.square(jax.nn.relu(h @ w_ff1[l])) @ w_ff2[l]
    return rmsnorm(x, final_g)
```

```python
import functools

import numpy as np
import jax
import jax.numpy as jnp
from jax import lax
from jax.experimental import pallas as pl
from jax.experimental.pallas import tpu as pltpu

F32 = jnp.float32
BF16 = jnp.bfloat16

D_MODEL = 1024
N_HEADS = 8
HEAD_DIM = 64
N_KV = 2
GROUP = N_HEADS // N_KV
ATTN_W = N_HEADS * HEAD_DIM
KV_W = 2 * N_KV * HEAD_DIM
CMP_BLOCK = 32
CMP_STRIDE = 16
CMP_HIDDEN = 64
SEL_BLOCK = 64
SEL_TOP_N = 16
WINDOW = 512
POOL_GROUPS = 4
POOL_DIM = 128
POOL_W = POOL_GROUPS * POOL_DIM
POOL_WINDOWS = (2, 4, 8, 16)
POOL_HALO = 16
D_FF = 4 * D_MODEL
EPS = 1e-6
NEG = -1e30
FORCE = 1e6
UNSEL = -1e9
SLOPES = tuple(2.0 ** (-8.0 * (h + 1) / N_HEADS) for h in range(N_HEADS))

LANES = 128
SUBLANES = 8
Q_BLOCK = 128
SEL_CHUNK = 512
TOK_TILE = 512
FF_CHUNK = 1024
VMEM_LIMIT = 56 * 1024 * 1024

_C_Q = 0
_C_KC = _C_Q + N_HEADS * LANES
_C_KS = _C_KC + KV_W
_C_VS = _C_KS + LANES
_C_KW = _C_VS + LANES
_C_VW = _C_KW + LANES
_C_GATE = _C_VW + LANES
_C_U = _C_GATE + LANES
_C_END = _C_U + POOL_W


def _dot(a, b):
    return jnp.dot(a, b, preferred_element_type=F32)


def _dot_nt(a, b):
    return lax.dot_general(a, b, (((1,), (1,)), ((), ())), preferred_element_type=F32)


def _rmsnorm(x, g):
    return x * lax.rsqrt(jnp.mean(x * x, axis=-1, keepdims=True) + EPS) * g


def _params(sem):
    return pltpu.CompilerParams(dimension_semantics=sem, vmem_limit_bytes=VMEM_LIMIT)


def _inproj_kernel(x_ref, g_ref, w_ref, qc_ref, ksc_ref, kwc_ref, vc_ref,
                   qa_ref, kvc_ref, ksa_ref, vsa_ref, kwa_ref, vwa_ref, gate_ref, u_ref):
    x = x_ref[...]
    h = _rmsnorm(x, g_ref[...]).astype(BF16)

    def proj(a, b):
        return _dot(h, w_ref[:, a:b])

    qa_ref[...] = (proj(_C_Q, _C_KC) + qc_ref[...]).astype(BF16)
    kvc_ref[...] = proj(_C_KC, _C_KS)
    low = lax.broadcasted_iota(jnp.int32, (x.shape[0], LANES), 1) < HEAD_DIM

    def split(r, c, out_ref):
        out_ref[0] = jnp.where(low, r, c).astype(BF16)
        out_ref[1] = jnp.where(low, pltpu.roll(r, HEAD_DIM, 1), c).astype(BF16)

    split(proj(_C_KS, _C_VS), ksc_ref[...], ksa_ref)
    split(proj(_C_VS, _C_KW), vc_ref[...], vsa_ref)
    split(proj(_C_KW, _C_VW), kwc_ref[...], kwa_ref)
    split(proj(_C_VW, _C_GATE), vc_ref[...], vwa_ref)
    gate_ref[...] = jax.nn.sigmoid(proj(_C_GATE, _C_U))
    u_ref[...] = proj(_C_U, _C_END)


def _inproj(x2, g1, w, qc, ksc, kwc, vc, S):
    T = x2.shape[0]
    TM = TOK_TILE
    nst = S // TM
    tok = lambda w_: pl.BlockSpec((TM, w_), lambda i: (i, 0))
    const = lambda shape: pl.BlockSpec(shape, lambda i: (0,) * len(shape))
    seq = pl.BlockSpec((TM, LANES), lambda i: (i % nst, 0))
    grp = pl.BlockSpec((N_KV, TM, LANES), lambda i: (0, i, 0))
    return pl.pallas_call(
        _inproj_kernel,
        grid=(T // TM,),
        in_specs=[tok(D_MODEL), const((1, D_MODEL)), const((D_MODEL, _C_END)),
                  const((1, N_HEADS * LANES)), seq, seq, const((1, LANES))],
        out_specs=[tok(N_HEADS * LANES), tok(KV_W), grp, grp, grp, grp, tok(LANES), tok(POOL_W)],
        out_shape=[jax.ShapeDtypeStruct((T, N_HEADS * LANES), BF16),
                   jax.ShapeDtypeStruct((T, KV_W), F32),
                   jax.ShapeDtypeStruct((N_KV, T, LANES), BF16),
                   jax.ShapeDtypeStruct((N_KV, T, LANES), BF16),
                   jax.ShapeDtypeStruct((N_KV, T, LANES), BF16),
                   jax.ShapeDtypeStruct((N_KV, T, LANES), BF16),
                   jax.ShapeDtypeStruct((T, LANES), F32),
                   jax.ShapeDtypeStruct((T, POOL_W), F32)],
        compiler_params=_params(("parallel",)),
    )(x2, g1, w, qc, ksc, kwc, vc)


def _compress_kernel(x_ref, pe_ref, w1_ref, w2_ref, cc_ref, kca_ref, vca_ref):
    x = x_ref[0]
    nc = x.shape[0]
    a = _dot((x + pe_ref[0:1, :]).astype(BF16), w1_ref[0])
    b = _dot((x + pe_ref[1:2, :]).astype(BF16), w1_ref[1])
    h = a + pltpu.roll(b, nc - 1, 0)
    h = jax.nn.gelu(h).astype(BF16)
    out = _dot(h, w2_ref[...])
    cc = cc_ref[...]
    for g in range(N_KV):
        kca_ref[0, g] = (out[:, g * LANES:(g + 1) * LANES] + cc).astype(BF16)
        vca_ref[0, g] = out[:, (N_KV + g) * LANES:(N_KV + g + 1) * LANES].astype(BF16)


def _compress(kvc3, pe, w1, w2, cc):
    B, NC, W = kvc3.shape
    const = lambda shape: pl.BlockSpec(shape, lambda b: (0,) * len(shape))
    out_spec = pl.BlockSpec((1, N_KV, NC, LANES), lambda b: (b, 0, 0, 0))
    return pl.pallas_call(
        _compress_kernel,
        grid=(B,),
        in_specs=[pl.BlockSpec((1, NC, W), lambda b: (b, 0, 0)), const((2, W)),
                  const((2, W, 2 * N_KV * CMP_HIDDEN)),
                  const((2 * N_KV * CMP_HIDDEN, 2 * N_KV * LANES)), const((NC, LANES))],
        out_specs=[out_spec, out_spec],
        out_shape=[jax.ShapeDtypeStruct((B, N_KV, NC, LANES), BF16)] * 2,
        compiler_params=_params(("parallel",)),
    )(kvc3, pe, w1, w2, cc)


def _nsa_kernel(qa_ref, gate_ref, kca_ref, vca_ref, ksa_ref, vsa_ref, kwa_ref, vwa_ref,
                ovt_ref, ssl_ref, o_ref, qsel_ref, m_ref, acc_ref, *, S):
    QB, R, M = Q_BLOCK, GROUP, GROUP * Q_BLOCK
    NC = S // CMP_STRIDE
    n_sel = S // SEL_BLOCK
    top_n = min(SEL_TOP_N, n_sel)
    WK = WINDOW + QB
    q0 = pl.program_id(1) * QB
    gates = gate_ref[...]

    tq_c = q0 + lax.broadcasted_iota(jnp.int32, (QB, NC), 0)
    ce = lax.broadcasted_iota(jnp.int32, (QB, NC), 1) * CMP_STRIDE + (CMP_BLOCK - 1)
    vis = tq_c >= ce
    visf = vis.astype(F32)
    vbias = jnp.where(vis, 0.0, NEG)

    jrow = lax.broadcasted_iota(jnp.int32, (SEL_BLOCK, QB), 0)
    cur = (q0 + lax.broadcasted_iota(jnp.int32, (SEL_BLOCK, QB), 1)) // SEL_BLOCK
    future = jrow > cur
    forced = (jrow == 0) | (jrow == cur) | (jrow == cur - 1)
    sub8 = lax.broadcasted_iota(jnp.int32, (SUBLANES, QB), 0)

    ws = pl.multiple_of(jnp.maximum(q0 - WINDOW, 0), QB)
    dw = (q0 + lax.broadcasted_iota(jnp.int32, (QB, WK), 0)) - (ws + lax.broadcasted_iota(jnp.int32, (QB, WK), 1))
    wbias = jnp.where(dw >= 0, jnp.where(dw < WINDOW, 0.0, NEG), NEG)

    nfull = q0 // SEL_CHUNK
    outs = []
    for g in range(N_KV):
        qa = jnp.concatenate(
            [qa_ref[:, (g * R + r) * LANES:(g * R + r + 1) * LANES] for r in range(R)], axis=0)

        s = _dot_nt(qa, kca_ref[0, g]).reshape(R, QB, NC) + vbias[None]
        m = jnp.max(s, axis=-1, keepdims=True)
        e = jnp.exp(s - m) * visf[None]
        l = jnp.sum(e, axis=-1, keepdims=True)
        p = e * (1.0 / jnp.maximum(l, 1e-30))
        o_c = _dot(p.reshape(M, NC).astype(BF16), vca_ref[0, g])

        psum = p[0] + p[1] + p[2] + p[3]
        hi = psum.astype(BF16)
        r1 = psum - hi.astype(F32)
        mid = r1.astype(BF16)
        lo = (r1 - mid.astype(F32)).astype(BF16)
        ovt = ovt_ref[...]
        imp = _dot_nt(ovt, hi) + _dot_nt(ovt, mid) + _dot_nt(ovt, lo)
        v = jnp.where(future, NEG, jnp.where(forced, FORCE, imp))
        vt = [v[k * SUBLANES:(k + 1) * SUBLANES] for k in range(SEL_BLOCK // SUBLANES)]
        cnt = [jnp.zeros((SUBLANES, QB), jnp.int32) for _ in vt]
        for i in range(n_sel):
            vi = v[i:i + 1, :]
            for k, vk in enumerate(vt):
                if (k + 1) * SUBLANES <= i + 1:
                    beats = jnp.where(vi > vk, 1, 0)
                elif k * SUBLANES > i:
                    beats = jnp.where(vi >= vk, 1, 0)
                else:
                    beats = jnp.where(sub8 > i - k * SUBLANES, jnp.where(vi >= vk, 1, 0),
                                      jnp.where(vi > vk, 1, 0))
                cnt[k] = cnt[k] + beats
        cnt = jnp.concatenate(cnt, axis=0)
        selb = jnp.where(cnt < top_n, jnp.where(v > NEG / 2, 0.0, UNSEL), UNSEL)
        aug = jnp.concatenate([jnp.zeros_like(selb), selb], axis=0).T
        for r in range(R):
            qh = qa[r * QB:(r + 1) * QB].astype(F32)
            qsel_ref[r * QB:(r + 1) * QB, :] = (qh + aug + ssl_ref[g * R + r:g * R + r + 1, :]).astype(BF16)

        m_ref[...] = jnp.full((M, 1), NEG, F32)
        acc_ref[...] = jnp.zeros((M, LANES), F32)

        def chunk(c, masked):
            k0 = pl.multiple_of(c * SEL_CHUNK, SEL_CHUNK)
            sc = _dot_nt(qsel_ref[...], ksa_ref[g, pl.ds(k0, SEL_CHUNK), :])
            if masked:
                kpos = k0 + lax.broadcasted_iota(jnp.int32, (QB, SEL_CHUNK), 1)
                tq = q0 + lax.broadcasted_iota(jnp.int32, (QB, SEL_CHUNK), 0)
                cb = jnp.where(kpos <= tq, 0.0, NEG)
                sc = (sc.reshape(R, QB, SEL_CHUNK) + cb[None]).reshape(M, SEL_CHUNK)
            m_prev = m_ref[...]
            m_new = jnp.maximum(m_prev, jnp.max(sc, axis=-1, keepdims=True))
            alpha = jnp.exp(m_prev - m_new)
            pc = jnp.exp(sc - m_new).astype(BF16)
            acc_ref[...] = alpha * acc_ref[...] + _dot(pc, vsa_ref[g, pl.ds(k0, SEL_CHUNK), :])
            m_ref[...] = m_new

        def body(c, carry):
            chunk(c, False)
            return carry

        lax.fori_loop(0, nfull, body, 0)
        chunk(nfull, True)
        acc = acc_ref[...]
        o_s = acc * (1.0 / acc[:, HEAD_DIM:HEAD_DIM + 1])

        sw = _dot_nt(qa, kwa_ref[g, pl.ds(ws, WK), :]).reshape(R, QB, WK) + wbias[None]
        mw = jnp.max(sw, axis=-1, keepdims=True)
        pw = jnp.exp(sw - mw).reshape(M, WK).astype(BF16)
        accw = _dot(pw, vwa_ref[g, pl.ds(ws, WK), :])
        o_w = accw * (1.0 / accw[:, HEAD_DIM:HEAD_DIM + 1])

        for r in range(R):
            col = (g * R + r) * 3
            rows = slice(r * QB, (r + 1) * QB)
            o = (gates[:, col:col + 1] * o_c[rows] + gates[:, col + 1:col + 2] * o_s[rows]
                 + gates[:, col + 2:col + 3] * o_w[rows])
            outs.append(o[:, :HEAD_DIM])
    o_ref[...] = jnp.concatenate(outs, axis=1).astype(BF16)


def _nsa(qa, gates, kca, vca, ksa, vsa, kwa, vwa, ovt, ssl, B, S):
    NC = S // CMP_STRIDE
    nqb = S // Q_BLOCK
    M = GROUP * Q_BLOCK
    tok = lambda w_: pl.BlockSpec((Q_BLOCK, w_), lambda b, i: (b * nqb + i, 0))
    cmp_spec = pl.BlockSpec((1, N_KV, NC, LANES), lambda b, i: (b, 0, 0, 0))
    seq_spec = pl.BlockSpec((N_KV, S, LANES), lambda b, i: (0, b, 0))
    const = lambda shape: pl.BlockSpec(shape, lambda b, i: (0,) * len(shape))
    return pl.pallas_call(
        functools.partial(_nsa_kernel, S=S),
        grid=(B, nqb),
        in_specs=[tok(N_HEADS * LANES), tok(LANES), cmp_spec, cmp_spec,
                  seq_spec, seq_spec, seq_spec, seq_spec,
                  const((SEL_BLOCK, NC)), const((N_HEADS, LANES))],
        out_specs=tok(ATTN_W),
        out_shape=jax.ShapeDtypeStruct((B * S, ATTN_W), BF16),
        scratch_shapes=[pltpu.VMEM((M, LANES), BF16), pltpu.VMEM((M, 1), F32),
                        pltpu.VMEM((M, LANES), F32)],
        compiler_params=_params(("parallel", "arbitrary")),
    )(qa, gates, kca, vca, ksa, vsa, kwa, vwa, ovt, ssl)


def _mix_kernel(x_ref, attn_ref, u_ref, uh_ref, g1_ref, wm_ref, wap_ref, wpool_ref, ps_ref,
                wpp_ref, wout_ref, o_ref, *, nst):
    TM = x_ref.shape[0]
    seq_tile = pl.program_id(0) % nst
    x = x_ref[...]
    h = _rmsnorm(x, g1_ref[...]).astype(BF16)
    merge = jax.nn.sigmoid(_dot(h, wm_ref[...]))
    a = _dot(attn_ref[...], wap_ref[...])

    halo = jnp.where(seq_tile == 0, 0.0, uh_ref[...])
    ue = jnp.concatenate([halo, u_ref[...]], axis=0)
    pos = seq_tile * TM + lax.broadcasted_iota(jnp.int32, (TM, 1), 0)
    ys = []
    for gi, w in enumerate(POOL_WINDOWS):
        ug = ue[:, gi * POOL_DIM:(gi + 1) * POOL_DIM]
        s = ug
        sh = 1
        while sh < w:
            s = s + pltpu.roll(s, sh, 0)
            sh *= 2
        cnt = jnp.minimum(pos + 1, w).astype(F32)
        p = s[POOL_HALO:] / cnt - ug[POOL_HALO:]
        ys.append(_dot(p.astype(BF16), wpool_ref[gi]))
    y = jnp.concatenate(ys, axis=1) * ps_ref[...]
    b = _dot(y.astype(BF16), wpp_ref[...])
    mix = merge[:, :D_MODEL] * a + merge[:, D_MODEL:] * b
    o_ref[...] = x + _dot(mix.astype(BF16), wout_ref[...])


def _mix(x2, attn, u, g1, wm, wap, wpool, ps, wpp, wout, S):
    T = x2.shape[0]
    TM = TOK_TILE
    nst = S // TM
    tok = lambda w_: pl.BlockSpec((TM, w_), lambda i: (i, 0))
    const = lambda shape: pl.BlockSpec(shape, lambda i: (0,) * len(shape))
    halo = pl.BlockSpec((POOL_HALO, POOL_W), lambda i: (jnp.maximum(i * (TM // POOL_HALO) - 1, 0), 0))
    return pl.pallas_call(
        functools.partial(_mix_kernel, nst=nst),
        grid=(T // TM,),
        in_specs=[tok(D_MODEL), tok(ATTN_W), tok(POOL_W), halo, const((1, D_MODEL)),
                  const((D_MODEL, 2 * D_MODEL)), const((ATTN_W, D_MODEL)),
                  const((POOL_GROUPS, POOL_DIM, POOL_DIM)), const((1, POOL_W)),
                  const((POOL_W, D_MODEL)), const((D_MODEL, D_MODEL))],
        out_specs=tok(D_MODEL),
        out_shape=jax.ShapeDtypeStruct((T, D_MODEL), F32),
        compiler_params=_params(("parallel",)),
    )(x2, attn, u, u, g1, wm, wap, wpool, ps, wpp, wout)


def _ffn_kernel(x_ref, g2_ref, w1_ref, w2_ref, gf_ref, o_ref, *, final):
    x = x_ref[...]
    h = _rmsnorm(x, g2_ref[...]).astype(BF16)
    acc = x
    for c in range(D_FF // FF_CHUNK):
        cols = slice(c * FF_CHUNK, (c + 1) * FF_CHUNK)
        t = jnp.square(jnp.maximum(_dot(h, w1_ref[:, cols]), 0.0)).astype(BF16)
        acc = acc + _dot(t, w2_ref[cols, :])
    if final:
        acc = _rmsnorm(acc, gf_ref[...])
    o_ref[...] = acc


def _ffn(x2, g2, w1, w2, gf, final):
    T = x2.shape[0]
    TM = TOK_TILE
    tok = pl.BlockSpec((TM, D_MODEL), lambda i: (i, 0))
    const = lambda shape: pl.BlockSpec(shape, lambda i: (0,) * len(shape))
    return pl.pallas_call(
        functools.partial(_ffn_kernel, final=final),
        grid=(T // TM,),
        in_specs=[tok, const((1, D_MODEL)), const((D_MODEL, D_FF)), const((D_FF, D_MODEL)),
                  const((1, D_MODEL))],
        out_specs=tok,
        out_shape=jax.ShapeDtypeStruct((T, D_MODEL), F32),
        compiler_params=_params(("parallel",)),
    )(x2, g2, w1, w2, gf)


def _pack_w_in(w_in_l):
    scale = HEAD_DIM ** -0.5
    wq = w_in_l[:, :ATTN_W].reshape(D_MODEL, N_HEADS, HEAD_DIM) * scale
    wq = jnp.pad(wq, ((0, 0), (0, 0), (0, LANES - HEAD_DIM))).reshape(D_MODEL, N_HEADS * LANES)
    o_kv = ATTN_W
    o_gate = o_kv + 3 * KV_W
    o_pool = o_gate + 3 * N_HEADS
    o_merge = o_pool + POOL_W
    wgate = jnp.pad(w_in_l[:, o_gate:o_pool], ((0, 0), (0, LANES - 3 * N_HEADS)))
    w = jnp.concatenate([wq, w_in_l[:, o_kv:o_gate], wgate, w_in_l[:, o_pool:o_merge]], axis=1)
    return w.astype(BF16), w_in_l[:, o_merge:].astype(BF16)


def _pack_compress(pe_l, w1_l, w2_l):
    eye = jnp.eye(N_KV, dtype=F32)
    half = CMP_BLOCK // 2
    w1r = w1_l.reshape(2, 2, half, HEAD_DIM, CMP_HIDDEN)
    w1big = jnp.einsum('khldm,kK,gG->hlkgdKGm', w1r, eye, eye)
    w1big = w1big.reshape(2, half * KV_W, 2 * N_KV * CMP_HIDDEN)
    per = pe_l.reshape(2, 2, half, HEAD_DIM)
    pebig = jnp.broadcast_to(per.transpose(1, 2, 0, 3)[:, :, :, None, :],
                             (2, half, 2, N_KV, HEAD_DIM)).reshape(2, half * KV_W)
    w2p = jnp.pad(w2_l, ((0, 0), (0, 0), (0, LANES - HEAD_DIM)))
    w2big = jnp.einsum('kme,kK,gG->kgmKGe', w2p, eye, eye)
    w2big = w2big.reshape(2 * N_KV * CMP_HIDDEN, 2 * N_KV * LANES)
    return pebig, w1big.astype(BF16), w2big.astype(BF16)


def _constants(S):
    NC = S // CMP_STRIDE
    n_sel = S // SEL_BLOCK
    pos = np.arange(S)
    ksc = np.zeros((S, LANES), np.float32)
    ksc[:, HEAD_DIM] = pos % SEL_BLOCK
    blk = pos // SEL_BLOCK
    sel_rows = blk >= 1
    ksc[pos[sel_rows], HEAD_DIM + blk[sel_rows]] = 1.0
    kwc = np.zeros((S, LANES), np.float32)
    kwc[:, HEAD_DIM] = pos % SEL_BLOCK
    kwc[:, HEAD_DIM + 1] = blk
    vc = np.zeros((1, LANES), np.float32)
    vc[0, HEAD_DIM] = 1.0
    qc = np.zeros((1, N_HEADS * LANES), np.float32)
    ssl = np.zeros((N_HEADS, LANES), np.float32)
    for h in range(N_HEADS):
        qc[0, h * LANES + HEAD_DIM] = SLOPES[h]
        qc[0, h * LANES + HEAD_DIM + 1] = SLOPES[h] * SEL_BLOCK
        ssl[h, HEAD_DIM + 2:] = SLOPES[h] * SEL_BLOCK * np.arange(2, SEL_BLOCK)
    c = np.arange(NC)
    cc = np.zeros((NC, LANES), np.float32)
    cstart = c * CMP_STRIDE
    cc[:, HEAD_DIM] = cstart % SEL_BLOCK
    cc[:, HEAD_DIM + 1] = cstart // SEL_BLOCK
    j = np.arange(SEL_BLOCK)
    ovt = ((cstart[None, :] <= j[:, None] * SEL_BLOCK + SEL_BLOCK - 1)
           & (cstart[None, :] + CMP_BLOCK - 1 >= j[:, None] * SEL_BLOCK)
           & (j[:, None] < n_sel) & (c[None, :] < NC - 1)).astype(np.float32)
    return (jnp.asarray(qc), jnp.asarray(ksc), jnp.asarray(kwc), jnp.asarray(vc), jnp.asarray(cc),
            jnp.asarray(ovt, BF16), jnp.asarray(ssl))


def kernel(x, norm1_g, w_in, cmp_pe, cmp_w1, cmp_w2, w_attn_proj, w_pool, pool_scale,
           w_pool_proj, w_out, norm2_g, w_ff1, w_ff2, final_g):
    B, S, _ = x.shape
    depth = w_in.shape[0]
    assert S % SEL_CHUNK == 0 and S % TOK_TILE == 0 and S // SEL_BLOCK <= SEL_BLOCK and S >= WINDOW + Q_BLOCK
    qc, ksc, kwc, vc, cc, ovt, ssl = _constants(S)
    x2 = x.reshape(B * S, D_MODEL)
    for l in range(depth):
        w_pack, w_merge = _pack_w_in(w_in[l])
        pebig, w1big, w2big = _pack_compress(cmp_pe[l], cmp_w1[l], cmp_w2[l])
        g1 = norm1_g[l].reshape(1, D_MODEL)
        qa, kvc, ksa, vsa, kwa, vwa, gates, u = _inproj(x2, g1, w_pack, qc, ksc, kwc, vc, S)
        kca, vca = _compress(kvc.reshape(B, S // CMP_STRIDE, CMP_STRIDE * KV_W), pebig, w1big, w2big, cc)
        attn = _nsa(qa, gates, kca, vca, ksa, vsa, kwa, vwa, ovt, ssl, B, S)
        x2 = _mix(x2, attn, u, g1, w_merge, w_attn_proj[l].astype(BF16), w_pool[l].astype(BF16),
                  pool_scale[l].reshape(1, POOL_W), w_pool_proj[l].astype(BF16),
                  w_out[l].astype(BF16), S)
        x2 = _ffn(x2, norm2_g[l].reshape(1, D_MODEL), w_ff1[l].astype(BF16), w_ff2[l].astype(BF16),
                  final_g.reshape(1, D_MODEL), final=(l == depth - 1))
    return x2.reshape(B, S, D_MODEL)
```

```python
import functools

import numpy as np
import jax
import jax.numpy as jnp
from jax import lax
from jax.experimental import pallas as pl
from jax.experimental.pallas import tpu as pltpu

F32 = jnp.float32
BF16 = jnp.bfloat16

D_MODEL = 1024
N_HEADS = 8
HEAD_DIM = 64
N_KV = 2
GROUP = N_HEADS // N_KV
ATTN_W = N_HEADS * HEAD_DIM
KV_W = 2 * N_KV * HEAD_DIM
CMP_BLOCK = 32
CMP_STRIDE = 16
CMP_HIDDEN = 64
SEL_BLOCK = 64
SEL_TOP_N = 16
WINDOW = 512
POOL_GROUPS = 4
POOL_DIM = 128
POOL_W = POOL_GROUPS * POOL_DIM
POOL_WINDOWS = (2, 4, 8, 16)
POOL_HALO = 16
D_FF = 4 * D_MODEL
EPS = 1e-6
NEG = -1e30
FORCE = 1e6
UNSEL = -1e9
SLOPES = tuple(2.0 ** (-8.0 * (h + 1) / N_HEADS) for h in range(N_HEADS))

LANES = 128
SUBLANES = 8
Q_BLOCK = 128
SEL_CHUNK = 512
TOK_TILE = 512
FF_CHUNK = 1024
VMEM_LIMIT = 56 * 1024 * 1024

_C_Q = 0
_C_KC = _C_Q + N_HEADS * LANES
_C_KS = _C_KC + KV_W
_C_VS = _C_KS + LANES
_C_KW = _C_VS + LANES
_C_VW = _C_KW + LANES
_C_GATE = _C_VW + LANES
_C_U = _C_GATE + LANES
_C_END = _C_U + POOL_W


def _dot(a, b):
    return jnp.dot(a, b, preferred_element_type=F32)


def _dot_nt(a, b):
    return lax.dot_general(a, b, (((1,), (1,)), ((), ())), preferred_element_type=F32)


def _rmsnorm(x, g):
    return x * lax.rsqrt(jnp.mean(x * x, axis=-1, keepdims=True) + EPS) * g


def _params(sem):
    return pltpu.CompilerParams(dimension_semantics=sem, vmem_limit_bytes=VMEM_LIMIT)


def _inproj_kernel(x_ref, g_ref, w_ref, qc_ref, ksc_ref, kwc_ref, vc_ref,
                   qa_ref, kvc_ref, ksa_ref, vst_ref, kwa_ref, vwt_ref, gate_ref, u_ref):
    x = x_ref[...]
    h = _rmsnorm(x, g_ref[...]).astype(BF16)

    def proj(a, b):
        return _dot(h, w_ref[:, a:b])

    qa_ref[...] = (proj(_C_Q, _C_KC) + qc_ref[...]).astype(BF16)
    kvc_ref[...] = proj(_C_KC, _C_KS)
    low = lax.broadcasted_iota(jnp.int32, (x.shape[0], LANES), 1) < HEAD_DIM

    def split(r, c):
        return jnp.where(low, r, c), jnp.where(low, pltpu.roll(r, HEAD_DIM, 1), c)

    for k_ref, v_ref, kc_ref, c0 in ((ksa_ref, vst_ref, ksc_ref, _C_KS), (kwa_ref, vwt_ref, kwc_ref, _C_KW)):
        k0, k1 = split(proj(c0, c0 + LANES), kc_ref[...])
        k_ref[0] = k0.astype(BF16)
        k_ref[1] = k1.astype(BF16)
        v0, v1 = split(proj(c0 + LANES, c0 + 2 * LANES), vc_ref[...])
        v_ref[0] = v0.T.astype(BF16)
        v_ref[1] = v1.T.astype(BF16)
    gate_ref[...] = jax.nn.sigmoid(proj(_C_GATE, _C_U))
    u_ref[...] = proj(_C_U, _C_END)


def _inproj(x2, g1, w, qc, ksc, kwc, vc, S):
    T = x2.shape[0]
    TM = TOK_TILE
    nst = S // TM
    tok = lambda w_: pl.BlockSpec((TM, w_), lambda i: (i, 0))
    const = lambda shape: pl.BlockSpec(shape, lambda i: (0,) * len(shape))
    seq = pl.BlockSpec((TM, LANES), lambda i: (i % nst, 0))
    grp = pl.BlockSpec((N_KV, TM, LANES), lambda i: (0, i, 0))
    grp_t = pl.BlockSpec((N_KV, LANES, TM), lambda i: (0, 0, i))
    return pl.pallas_call(
        _inproj_kernel,
        grid=(T // TM,),
        in_specs=[tok(D_MODEL), const((1, D_MODEL)), const((D_MODEL, _C_END)),
                  const((1, N_HEADS * LANES)), seq, seq, const((1, LANES))],
        out_specs=[tok(N_HEADS * LANES), tok(KV_W), grp, grp_t, grp, grp_t, tok(LANES), tok(POOL_W)],
        out_shape=[jax.ShapeDtypeStruct((T, N_HEADS * LANES), BF16),
                   jax.ShapeDtypeStruct((T, KV_W), F32),
                   jax.ShapeDtypeStruct((N_KV, T, LANES), BF16),
                   jax.ShapeDtypeStruct((N_KV, LANES, T), BF16),
                   jax.ShapeDtypeStruct((N_KV, T, LANES), BF16),
                   jax.ShapeDtypeStruct((N_KV, LANES, T), BF16),
                   jax.ShapeDtypeStruct((T, LANES), F32),
                   jax.ShapeDtypeStruct((T, POOL_W), F32)],
        compiler_params=_params(("parallel",)),
    )(x2, g1, w, qc, ksc, kwc, vc)


def _compress_kernel(x_ref, pe_ref, w1_ref, w2_ref, cc_ref, kca_ref, vct_ref):
    x = x_ref[0]
    nc = x.shape[0]
    a = _dot((x + pe_ref[0:1, :]).astype(BF16), w1_ref[0])
    b = _dot((x + pe_ref[1:2, :]).astype(BF16), w1_ref[1])
    h = a + pltpu.roll(b, nc - 1, 0)
    h = jax.nn.gelu(h).astype(BF16)
    out = _dot(h, w2_ref[...])
    cc = cc_ref[...]
    for g in range(N_KV):
        kca_ref[0, g] = (out[:, g * LANES:(g + 1) * LANES] + cc).astype(BF16)
        vct_ref[0, g] = out[:, (N_KV + g) * LANES:(N_KV + g + 1) * LANES].T.astype(BF16)


def _compress(kvc3, pe, w1, w2, cc):
    B, NC, W = kvc3.shape
    const = lambda shape: pl.BlockSpec(shape, lambda b: (0,) * len(shape))
    return pl.pallas_call(
        _compress_kernel,
        grid=(B,),
        in_specs=[pl.BlockSpec((1, NC, W), lambda b: (b, 0, 0)), const((2, W)),
                  const((2, W, 2 * N_KV * CMP_HIDDEN)),
                  const((2 * N_KV * CMP_HIDDEN, 2 * N_KV * LANES)), const((NC, LANES))],
        out_specs=[pl.BlockSpec((1, N_KV, NC, LANES), lambda b: (b, 0, 0, 0)),
                   pl.BlockSpec((1, N_KV, LANES, NC), lambda b: (b, 0, 0, 0))],
        out_shape=[jax.ShapeDtypeStruct((B, N_KV, NC, LANES), BF16),
                   jax.ShapeDtypeStruct((B, N_KV, LANES, NC), BF16)],
        compiler_params=_params(("parallel",)),
    )(kvc3, pe, w1, w2, cc)


def _nsa_kernel(qa_ref, gate_ref, kca_ref, vct_ref, ksa_ref, vst_ref, kwa_ref, vwt_ref,
                ov_ref, ssl_ref, o_ref, qsel_ref, m_ref, acc_ref, s0_ref, pp_ref, pa_ref, *, S):
    QB, R, M = Q_BLOCK, GROUP, GROUP * Q_BLOCK
    NC = S // CMP_STRIDE
    n_sel = S // SEL_BLOCK
    top_n = min(SEL_TOP_N, n_sel)
    WK = WINDOW + QB
    q0 = pl.program_id(1) * QB

    def heads(a):
        return jnp.concatenate([a] * R, axis=1)

    ce = lax.broadcasted_iota(jnp.int32, (NC, QB), 0) * CMP_STRIDE + (CMP_BLOCK - 1)
    vis = (q0 + lax.broadcasted_iota(jnp.int32, (NC, QB), 1)) >= ce
    visf = heads(vis.astype(F32))
    vbias = heads(jnp.where(vis, 0.0, NEG))

    jrow = lax.broadcasted_iota(jnp.int32, (SEL_BLOCK, QB), 0)
    cur = (q0 + lax.broadcasted_iota(jnp.int32, (SEL_BLOCK, QB), 1)) // SEL_BLOCK
    future = jrow > cur
    forced = (jrow == 0) | (jrow == cur) | (jrow == cur - 1)
    sub8 = lax.broadcasted_iota(jnp.int32, (SUBLANES, QB), 0)

    ws = pl.multiple_of(jnp.maximum(q0 - WINDOW, 0), QB)
    dw = (q0 + lax.broadcasted_iota(jnp.int32, (WK, QB), 1)) - (ws + lax.broadcasted_iota(jnp.int32, (WK, QB), 0))
    wbias = heads(jnp.where(dw >= 0, jnp.where(dw < WINDOW, 0.0, NEG), NEG))

    qas = [jnp.concatenate([qa_ref[:, (g * R + r) * LANES:(g * R + r + 1) * LANES] for r in range(R)],
                           axis=0) for g in range(N_KV)]
    s_cs = [_dot_nt(kca_ref[0, g], qas[g]) for g in range(N_KV)]
    s_ws = [_dot_nt(kwa_ref[g, pl.ds(ws, WK), :], qas[g]) for g in range(N_KV)]

    o_cs, imps = [], []
    for g in range(N_KV):
        s = s_cs[g] + vbias
        m = jnp.max(s, axis=0, keepdims=True)
        e = jnp.exp(s - m) * visf
        l = jnp.sum(e, axis=0, keepdims=True)
        p = e * (1.0 / jnp.maximum(l, 1e-30))
        o_cs.append(_dot(vct_ref[0, g], p.astype(BF16)))
        psum = p[:, 0:QB] + p[:, QB:2 * QB] + p[:, 2 * QB:3 * QB] + p[:, 3 * QB:4 * QB]
        hi = psum.astype(BF16)
        r1 = psum - hi.astype(F32)
        mid = r1.astype(BF16)
        lo = (r1 - mid.astype(F32)).astype(BF16)
        ov = ov_ref[...]
        imps.append(_dot(ov, hi) + _dot(ov, mid) + _dot(ov, lo))

    def select(g):
        v = jnp.where(future, NEG, jnp.where(forced, FORCE, imps[g]))
        vt = [v[k * SUBLANES:(k + 1) * SUBLANES] for k in range(SEL_BLOCK // SUBLANES)]
        cnt = [jnp.zeros((SUBLANES, QB), jnp.int32) for _ in vt]
        for i in range(n_sel):
            vi = v[i:i + 1, :]
            for k, vk in enumerate(vt):
                if (k + 1) * SUBLANES <= i + 1:
                    beats = jnp.where(vi > vk, 1, 0)
                elif k * SUBLANES > i:
                    beats = jnp.where(vi >= vk, 1, 0)
                else:
                    beats = jnp.where(sub8 > i - k * SUBLANES, jnp.where(vi >= vk, 1, 0),
                                      jnp.where(vi > vk, 1, 0))
                cnt[k] = cnt[k] + beats
        cnt = jnp.concatenate(cnt, axis=0)
        selb = jnp.where(cnt < top_n, jnp.where(v > NEG / 2, 0.0, UNSEL), UNSEL)
        aug = jnp.concatenate([jnp.zeros_like(selb), selb], axis=0).T
        for r in range(R):
            qh = qas[g][r * QB:(r + 1) * QB].astype(F32)
            qsel_ref[g, r * QB:(r + 1) * QB, :] = (
                qh + aug + ssl_ref[g * R + r:g * R + r + 1, :]).astype(BF16)

    def window(g):
        sw = s_ws[g] + wbias
        pw = jnp.exp(sw - jnp.max(sw, axis=0, keepdims=True)).astype(BF16)
        accw = _dot(vwt_ref[g, :, pl.ds(ws, WK)], pw)
        return accw * (1.0 / accw[HEAD_DIM:HEAD_DIM + 1, :])

    def sel_scores(g, k0):
        return _dot_nt(ksa_ref[g, pl.ds(k0, SEL_CHUNK), :], qsel_ref[g])

    select(0)
    s0_ref[...] = sel_scores(0, 0)
    o_ws = [window(0)]
    select(1)
    o_ws.append(window(1))

    m_ref[...] = jnp.full(m_ref.shape, NEG, F32)
    acc_ref[...] = jnp.zeros(acc_ref.shape, F32)
    pp_ref[...] = jnp.zeros(pp_ref.shape, BF16)
    pa_ref[...] = jnp.ones(pa_ref.shape, F32)

    def online(g, sc):
        m_prev = m_ref[g]
        m_new = jnp.maximum(m_prev, jnp.max(sc, axis=0, keepdims=True))
        m_ref[g] = m_new
        return jnp.exp(sc - m_new).astype(BF16), jnp.exp(m_prev - m_new)

    def pending_pv(k0):
        acc_ref[1] = pa_ref[...] * acc_ref[1] + _dot(vst_ref[1, :, pl.ds(k0, SEL_CHUNK)], pp_ref[...])

    def step(c, last):
        k0 = pl.multiple_of(c * SEL_CHUNK, SEL_CHUNK)
        kprev = pl.multiple_of(jnp.maximum(c - 1, 0) * SEL_CHUNK, SEL_CHUNK)
        s1 = sel_scores(1, k0)
        pending_pv(kprev)
        s0 = s0_ref[...]
        if last:
            kpos = k0 + lax.broadcasted_iota(jnp.int32, (SEL_CHUNK, QB), 0)
            tq = q0 + lax.broadcasted_iota(jnp.int32, (SEL_CHUNK, QB), 1)
            cb = heads(jnp.where(kpos <= tq, 0.0, NEG))
            s0 = s0 + cb
            s1 = s1 + cb
        p0, alpha0 = online(0, s0)
        acc_ref[0] = alpha0 * acc_ref[0] + _dot(vst_ref[0, :, pl.ds(k0, SEL_CHUNK)], p0)
        if not last:
            s0_ref[...] = sel_scores(0, pl.multiple_of(k0 + SEL_CHUNK, SEL_CHUNK))
        p1, alpha1 = online(1, s1)
        pp_ref[...] = p1
        pa_ref[...] = alpha1

    def body(c, carry):
        step(c, False)
        return carry

    nfull = q0 // SEL_CHUNK
    lax.fori_loop(0, nfull, body, 0)
    step(nfull, True)
    pending_pv(pl.multiple_of(nfull * SEL_CHUNK, SEL_CHUNK))

    gt = gate_ref[...].T
    outs = []
    for g in range(N_KV):
        acc = acc_ref[g]
        o_s = acc * (1.0 / acc[HEAD_DIM:HEAD_DIM + 1, :])
        o_c, o_w = o_cs[g], o_ws[g]
        for r in range(R):
            col = (g * R + r) * 3
            cols = slice(r * QB, (r + 1) * QB)
            o = (gt[col:col + 1] * o_c[:HEAD_DIM, cols] + gt[col + 1:col + 2] * o_s[:HEAD_DIM, cols]
                 + gt[col + 2:col + 3] * o_w[:HEAD_DIM, cols])
            outs.append(o)
    o_ref[...] = jnp.concatenate(outs, axis=0).T.astype(BF16)


def _nsa(qa, gates, kca, vct, ksa, vst, kwa, vwt, ov, ssl, B, S):
    NC = S // CMP_STRIDE
    nqb = S // Q_BLOCK
    M = GROUP * Q_BLOCK
    tok = lambda w_: pl.BlockSpec((Q_BLOCK, w_), lambda b, i: (b * nqb + i, 0))
    const = lambda shape: pl.BlockSpec(shape, lambda b, i: (0,) * len(shape))
    seq = pl.BlockSpec((N_KV, S, LANES), lambda b, i: (0, b, 0))
    seq_t = pl.BlockSpec((N_KV, LANES, S), lambda b, i: (0, 0, b))
    return pl.pallas_call(
        functools.partial(_nsa_kernel, S=S),
        grid=(B, nqb),
        in_specs=[tok(N_HEADS * LANES), tok(LANES),
                  pl.BlockSpec((1, N_KV, NC, LANES), lambda b, i: (b, 0, 0, 0)),
                  pl.BlockSpec((1, N_KV, LANES, NC), lambda b, i: (b, 0, 0, 0)),
                  seq, seq_t, seq, seq_t,
                  const((SEL_BLOCK, NC)), const((N_HEADS, LANES))],
        out_specs=tok(ATTN_W),
        out_shape=jax.ShapeDtypeStruct((B * S, ATTN_W), BF16),
        scratch_shapes=[pltpu.VMEM((N_KV, M, LANES), BF16), pltpu.VMEM((N_KV, 1, M), F32),
                        pltpu.VMEM((N_KV, LANES, M), F32), pltpu.VMEM((SEL_CHUNK, M), F32),
                        pltpu.VMEM((SEL_CHUNK, M), BF16), pltpu.VMEM((1, M), F32)],
        compiler_params=_params(("parallel", "arbitrary")),
    )(qa, gates, kca, vct, ksa, vst, kwa, vwt, ov, ssl)


def _mix_kernel(x_ref, attn_ref, u_ref, uh_ref, g1_ref, wm_ref, wap_ref, wpool_ref, ps_ref,
                wpp_ref, wout_ref, o_ref, *, nst):
    TM = x_ref.shape[0]
    seq_tile = pl.program_id(0) % nst
    x = x_ref[...]
    h = _rmsnorm(x, g1_ref[...]).astype(BF16)
    merge = jax.nn.sigmoid(_dot(h, wm_ref[...]))
    a = _dot(attn_ref[...], wap_ref[...])

    halo = jnp.where(seq_tile == 0, 0.0, uh_ref[...])
    ue = jnp.concatenate([halo, u_ref[...]], axis=0)
    pos = seq_tile * TM + lax.broadcasted_iota(jnp.int32, (TM, 1), 0)
    ys = []
    for gi, w in enumerate(POOL_WINDOWS):
        ug = ue[:, gi * POOL_DIM:(gi + 1) * POOL_DIM]
        s = ug
        sh = 1
        while sh < w:
            s = s + pltpu.roll(s, sh, 0)
            sh *= 2
        cnt = jnp.minimum(pos + 1, w).astype(F32)
        p = s[POOL_HALO:] / cnt - ug[POOL_HALO:]
        ys.append(_dot(p.astype(BF16), wpool_ref[gi]))
    y = jnp.concatenate(ys, axis=1) * ps_ref[...]
    b = _dot(y.astype(BF16), wpp_ref[...])
    mix = merge[:, :D_MODEL] * a + merge[:, D_MODEL:] * b
    o_ref[...] = x + _dot(mix.astype(BF16), wout_ref[...])


def _mix(x2, attn, u, g1, wm, wap, wpool, ps, wpp, wout, S):
    T = x2.shape[0]
    TM = TOK_TILE
    nst = S // TM
    tok = lambda w_: pl.BlockSpec((TM, w_), lambda i: (i, 0))
    const = lambda shape: pl.BlockSpec(shape, lambda i: (0,) * len(shape))
    halo = pl.BlockSpec((POOL_HALO, POOL_W), lambda i: (jnp.maximum(i * (TM // POOL_HALO) - 1, 0), 0))
    return pl.pallas_call(
        functools.partial(_mix_kernel, nst=nst),
        grid=(T // TM,),
        in_specs=[tok(D_MODEL), tok(ATTN_W), tok(POOL_W), halo, const((1, D_MODEL)),
                  const((D_MODEL, 2 * D_MODEL)), const((ATTN_W, D_MODEL)),
                  const((POOL_GROUPS, POOL_DIM, POOL_DIM)), const((1, POOL_W)),
                  const((POOL_W, D_MODEL)), const((D_MODEL, D_MODEL))],
        out_specs=tok(D_MODEL),
        out_shape=jax.ShapeDtypeStruct((T, D_MODEL), F32),
        compiler_params=_params(("parallel",)),
    )(x2, attn, u, u, g1, wm, wap, wpool, ps, wpp, wout)


def _ffn_kernel(x_ref, g2_ref, w1_ref, w2_ref, gf_ref, o_ref, *, final):
    x = x_ref[...]
    h = _rmsnorm(x, g2_ref[...]).astype(BF16)
    acc = x
    for c in range(D_FF // FF_CHUNK):
        cols = slice(c * FF_CHUNK, (c + 1) * FF_CHUNK)
        t = jnp.square(jnp.maximum(_dot(h, w1_ref[:, cols]), 0.0)).astype(BF16)
        acc = acc + _dot(t, w2_ref[cols, :])
    if final:
        acc = _rmsnorm(acc, gf_ref[...])
    o_ref[...] = acc


def _ffn(x2, g2, w1, w2, gf, final):
    T = x2.shape[0]
    TM = TOK_TILE
    tok = pl.BlockSpec((TM, D_MODEL), lambda i: (i, 0))
    const = lambda shape: pl.BlockSpec(shape, lambda i: (0,) * len(shape))
    return pl.pallas_call(
        functools.partial(_ffn_kernel, final=final),
        grid=(T // TM,),
        in_specs=[tok, const((1, D_MODEL)), const((D_MODEL, D_FF)), const((D_FF, D_MODEL)),
                  const((1, D_MODEL))],
        out_specs=tok,
        out_shape=jax.ShapeDtypeStruct((T, D_MODEL), F32),
        compiler_params=_params(("parallel",)),
    )(x2, g2, w1, w2, gf)


def _pack_w_in(w_in_l):
    scale = HEAD_DIM ** -0.5
    wq = w_in_l[:, :ATTN_W].reshape(D_MODEL, N_HEADS, HEAD_DIM) * scale
    wq = jnp.pad(wq, ((0, 0), (0, 0), (0, LANES - HEAD_DIM))).reshape(D_MODEL, N_HEADS * LANES)
    o_kv = ATTN_W
    o_gate = o_kv + 3 * KV_W
    o_pool = o_gate + 3 * N_HEADS
    o_merge = o_pool + POOL_W
    wgate = jnp.pad(w_in_l[:, o_gate:o_pool], ((0, 0), (0, LANES - 3 * N_HEADS)))
    w = jnp.concatenate([wq, w_in_l[:, o_kv:o_gate], wgate, w_in_l[:, o_pool:o_merge]], axis=1)
    return w.astype(BF16), w_in_l[:, o_merge:].astype(BF16)


def _pack_compress(pe_l, w1_l, w2_l):
    eye = jnp.eye(N_KV, dtype=F32)
    half = CMP_BLOCK // 2
    w1r = w1_l.reshape(2, 2, half, HEAD_DIM, CMP_HIDDEN)
    w1big = jnp.einsum('khldm,kK,gG->hlkgdKGm', w1r, eye, eye)
    w1big = w1big.reshape(2, half * KV_W, 2 * N_KV * CMP_HIDDEN)
    per = pe_l.reshape(2, 2, half, HEAD_DIM)
    pebig = jnp.broadcast_to(per.transpose(1, 2, 0, 3)[:, :, :, None, :],
                             (2, half, 2, N_KV, HEAD_DIM)).reshape(2, half * KV_W)
    w2p = jnp.pad(w2_l, ((0, 0), (0, 0), (0, LANES - HEAD_DIM)))
    w2big = jnp.einsum('kme,kK,gG->kgmKGe', w2p, eye, eye)
    w2big = w2big.reshape(2 * N_KV * CMP_HIDDEN, 2 * N_KV * LANES)
    return pebig, w1big.astype(BF16), w2big.astype(BF16)


def _constants(S):
    NC = S // CMP_STRIDE
    n_sel = S // SEL_BLOCK
    pos = np.arange(S)
    ksc = np.zeros((S, LANES), np.float32)
    ksc[:, HEAD_DIM] = pos % SEL_BLOCK
    blk = pos // SEL_BLOCK
    sel_rows = blk >= 1
    ksc[pos[sel_rows], HEAD_DIM + blk[sel_rows]] = 1.0
    kwc = np.zeros((S, LANES), np.float32)
    kwc[:, HEAD_DIM] = pos % SEL_BLOCK
    kwc[:, HEAD_DIM + 1] = blk
    vc = np.zeros((1, LANES), np.float32)
    vc[0, HEAD_DIM] = 1.0
    qc = np.zeros((1, N_HEADS * LANES), np.float32)
    ssl = np.zeros((N_HEADS, LANES), np.float32)
    for h in range(N_HEADS):
        qc[0, h * LANES + HEAD_DIM] = SLOPES[h]
        qc[0, h * LANES + HEAD_DIM + 1] = SLOPES[h] * SEL_BLOCK
        ssl[h, HEAD_DIM + 2:] = SLOPES[h] * SEL_BLOCK * np.arange(2, SEL_BLOCK)
    c = np.arange(NC)
    cc = np.zeros((NC, LANES), np.float32)
    cstart = c * CMP_STRIDE
    cc[:, HEAD_DIM] = cstart % SEL_BLOCK
    cc[:, HEAD_DIM + 1] = cstart // SEL_BLOCK
    j = np.arange(SEL_BLOCK)
    ovt = ((cstart[None, :] <= j[:, None] * SEL_BLOCK + SEL_BLOCK - 1)
           & (cstart[None, :] + CMP_BLOCK - 1 >= j[:, None] * SEL_BLOCK)
           & (j[:, None] < n_sel) & (c[None, :] < NC - 1)).astype(np.float32)
    return (jnp.asarray(qc), jnp.asarray(ksc), jnp.asarray(kwc), jnp.asarray(vc), jnp.asarray(cc),
            jnp.asarray(ovt, BF16), jnp.asarray(ssl))


def kernel(x, norm1_g, w_in, cmp_pe, cmp_w1, cmp_w2, w_attn_proj, w_pool, pool_scale,
           w_pool_proj, w_out, norm2_g, w_ff1, w_ff2, final_g):
    B, S, _ = x.shape
    depth = w_in.shape[0]
    assert S % SEL_CHUNK == 0 and S % TOK_TILE == 0 and S // SEL_BLOCK <= SEL_BLOCK and S >= WINDOW + Q_BLOCK
    qc, ksc, kwc, vc, cc, ovt, ssl = _constants(S)
    x2 = x.reshape(B * S, D_MODEL)
    for l in range(depth):
        w_pack, w_merge = _pack_w_in(w_in[l])
        pebig, w1big, w2big = _pack_compress(cmp_pe[l], cmp_w1[l], cmp_w2[l])
        g1 = norm1_g[l].reshape(1, D_MODEL)
        qa, kvc, ksa, vst, kwa, vwt, gates, u = _inproj(x2, g1, w_pack, qc, ksc, kwc, vc, S)
        kca, vct = _compress(kvc.reshape(B, S // CMP_STRIDE, CMP_STRIDE * KV_W), pebig, w1big, w2big, cc)
        attn = _nsa(qa, gates, kca, vct, ksa, vst, kwa, vwt, ovt, ssl, B, S)
        x2 = _mix(x2, attn, u, g1, w_merge, w_attn_proj[l].astype(BF16), w_pool[l].astype(BF16),
                  pool_scale[l].reshape(1, POOL_W), w_pool_proj[l].astype(BF16),
                  w_out[l].astype(BF16), S)
        x2 = _ffn(x2, norm2_g[l].reshape(1, D_MODEL), w_ff1[l].astype(BF16), w_ff2[l].astype(BF16),
                  final_g.reshape(1, D_MODEL), final=(l == depth - 1))
    return x2.reshape(B, S, D_MODEL)
```

```python
import functools

import numpy as np
import jax
import jax.numpy as jnp
from jax import lax
from jax.experimental import pallas as pl
from jax.experimental.pallas import tpu as pltpu

F32 = jnp.float32
BF16 = jnp.bfloat16

D_MODEL = 1024
N_HEADS = 8
HEAD_DIM = 64
N_KV = 2
GROUP = N_HEADS // N_KV
ATTN_W = N_HEADS * HEAD_DIM
KV_W = 2 * N_KV * HEAD_DIM
CMP_BLOCK = 32
CMP_STRIDE = 16
CMP_HIDDEN = 64
SEL_BLOCK = 64
SEL_TOP_N = 16
WINDOW = 512
POOL_GROUPS = 4
POOL_DIM = 128
POOL_W = POOL_GROUPS * POOL_DIM
POOL_WINDOWS = (2, 4, 8, 16)
POOL_HALO = 16
D_FF = 4 * D_MODEL
EPS = 1e-6
NEG = -1e30
FORCE = 1e6
UNSEL = -1e9
SLOPES = tuple(2.0 ** (-8.0 * (h + 1) / N_HEADS) for h in range(N_HEADS))

LANES = 128
SUBLANES = 8
Q_BLOCK = 256
SEL_CHUNK = 512
TOK_TILE = 512
FF_CHUNK = 1024
VMEM_LIMIT = 56 * 1024 * 1024

_C_Q = 0
_C_KC = _C_Q + ATTN_W
_C_KS = _C_KC + KV_W
_C_VS = _C_KS + LANES
_C_KW = _C_VS + LANES
_C_VW = _C_KW + LANES
_C_GATE = _C_VW + LANES
_C_U = _C_GATE + LANES
_C_END = _C_U + POOL_W


def _dot(a, b):
    return jnp.dot(a, b, preferred_element_type=F32)


def _dot_nt(a, b):
    return lax.dot_general(a, b, (((1,), (1,)), ((), ())), preferred_element_type=F32)


def _rmsnorm(x, g):
    return x * lax.rsqrt(jnp.mean(x * x, axis=-1, keepdims=True) + EPS) * g


def _params(sem):
    return pltpu.CompilerParams(dimension_semantics=sem, vmem_limit_bytes=VMEM_LIMIT)


def _inproj_kernel(x_ref, g_ref, w_ref, qc_ref, ksc_ref, kwc_ref, vc_ref,
                   qa_ref, kvc_ref, ksa_ref, vst_ref, kwa_ref, vwt_ref, gate_ref, u_ref):
    x = x_ref[...]
    h = _rmsnorm(x, g_ref[...]).astype(BF16)

    def proj(a, b):
        return _dot(h, w_ref[:, a:b])

    kvc_ref[0] = proj(_C_KC, _C_KC + LANES)
    kvc_ref[1] = proj(_C_KC + LANES, _C_KS)
    low = lax.broadcasted_iota(jnp.int32, (x.shape[0], LANES), 1) < HEAD_DIM

    def split(r, c0, c1):
        return jnp.where(low, r, c0), jnp.where(low, pltpu.roll(r, HEAD_DIM, 1), c1)

    rq = proj(_C_Q, _C_KC)
    for j in range(N_HEADS // 2):
        lanes = [slice((2 * j + i) * LANES, (2 * j + i + 1) * LANES) for i in range(2)]
        q0, q1 = split(rq[:, j * LANES:(j + 1) * LANES], qc_ref[:, lanes[0]], qc_ref[:, lanes[1]])
        qa_ref[:, lanes[0]] = q0.astype(BF16)
        qa_ref[:, lanes[1]] = q1.astype(BF16)
    for k_ref, v_ref, kc_ref, c0 in ((ksa_ref, vst_ref, ksc_ref, _C_KS), (kwa_ref, vwt_ref, kwc_ref, _C_KW)):
        k0, k1 = split(proj(c0, c0 + LANES), kc_ref[...], kc_ref[...])
        k_ref[0] = k0.astype(BF16)
        k_ref[1] = k1.astype(BF16)
        v0, v1 = split(proj(c0 + LANES, c0 + 2 * LANES), vc_ref[...], vc_ref[...])
        v_ref[0] = v0.T.astype(BF16)
        v_ref[1] = v1.T.astype(BF16)
    gate_ref[...] = jax.nn.sigmoid(proj(_C_GATE, _C_U))
    u_ref[...] = proj(_C_U, _C_END)


def _inproj(x2, g1, w, qc, ksc, kwc, vc, S):
    T = x2.shape[0]
    TM = TOK_TILE
    nst = S // TM
    tok = lambda w_: pl.BlockSpec((TM, w_), lambda i: (i, 0))
    const = lambda shape: pl.BlockSpec(shape, lambda i: (0,) * len(shape))
    seq = pl.BlockSpec((TM, LANES), lambda i: (i % nst, 0))
    grp = pl.BlockSpec((N_KV, TM, LANES), lambda i: (0, i, 0))
    grp_t = pl.BlockSpec((N_KV, LANES, TM), lambda i: (0, 0, i))
    return pl.pallas_call(
        _inproj_kernel,
        grid=(T // TM,),
        in_specs=[tok(D_MODEL), const((1, D_MODEL)), const((D_MODEL, _C_END)),
                  const((1, N_HEADS * LANES)), seq, seq, const((1, LANES))],
        out_specs=[tok(N_HEADS * LANES), pl.BlockSpec((2, TM, LANES), lambda i: (0, i, 0)),
                   grp, grp_t, grp, grp_t, tok(LANES), tok(POOL_W)],
        out_shape=[jax.ShapeDtypeStruct((T, N_HEADS * LANES), BF16),
                   jax.ShapeDtypeStruct((2, T, LANES), F32),
                   jax.ShapeDtypeStruct((N_KV, T, LANES), BF16),
                   jax.ShapeDtypeStruct((N_KV, LANES, T), BF16),
                   jax.ShapeDtypeStruct((N_KV, T, LANES), BF16),
                   jax.ShapeDtypeStruct((N_KV, LANES, T), BF16),
                   jax.ShapeDtypeStruct((T, LANES), F32),
                   jax.ShapeDtypeStruct((T, POOL_W), F32)],
        compiler_params=_params(("parallel",)),
    )(x2, g1, w, qc, ksc, kwc, vc)


def _compress_kernel(x_ref, pe_ref, w1_ref, w2_ref, cc_ref, kca_ref, vct_ref):
    nc = x_ref.shape[1] // CMP_STRIDE
    hs = []
    for kv in range(2):
        a = jnp.zeros((nc, LANES), F32)
        b = jnp.zeros((nc, LANES), F32)
        for l in range(CMP_STRIDE):
            xl = x_ref[kv, pl.ds(l, nc, stride=CMP_STRIDE), :]
            a = a + _dot((xl + pe_ref[0, l, kv:kv + 1, :]).astype(BF16), w1_ref[0, l, kv])
            b = b + _dot((xl + pe_ref[1, l, kv:kv + 1, :]).astype(BF16), w1_ref[1, l, kv])
        hs.append(a + pltpu.roll(b, nc - 1, 0))
    h = jnp.concatenate(hs, axis=1)
    h = jax.nn.gelu(h).astype(BF16)
    out = _dot(h, w2_ref[...])
    cc = cc_ref[...]
    for g in range(N_KV):
        kca_ref[0, g] = (out[:, g * LANES:(g + 1) * LANES] + cc).astype(BF16)
        vct_ref[0, g] = out[:, (N_KV + g) * LANES:(N_KV + g + 1) * LANES].T.astype(BF16)


def _compress(kvc3, pe, w1, w2, cc, B):
    S = kvc3.shape[1] // B
    NC = S // CMP_STRIDE
    const = lambda shape: pl.BlockSpec(shape, lambda b: (0,) * len(shape))
    return pl.pallas_call(
        _compress_kernel,
        grid=(B,),
        in_specs=[pl.BlockSpec((2, S, LANES), lambda b: (0, b, 0)), const((2, CMP_STRIDE, 2, LANES)),
                  const((2, CMP_STRIDE, 2, LANES, LANES)),
                  const((2 * N_KV * CMP_HIDDEN, 2 * N_KV * LANES)), const((NC, LANES))],
        out_specs=[pl.BlockSpec((1, N_KV, NC, LANES), lambda b: (b, 0, 0, 0)),
                   pl.BlockSpec((1, N_KV, LANES, NC), lambda b: (b, 0, 0, 0))],
        out_shape=[jax.ShapeDtypeStruct((B, N_KV, NC, LANES), BF16),
                   jax.ShapeDtypeStruct((B, N_KV, LANES, NC), BF16)],
        compiler_params=_params(("parallel",)),
    )(kvc3, pe, w1, w2, cc)


def _nsa_kernel(qa_ref, gate_ref, kca_ref, vct_ref, ksa_ref, vst_ref, kwa_ref, vwt_ref,
                ov_ref, ssl_ref, o_ref, qsel_ref, m_ref, acc_ref, pp_ref, pa_ref, s0a_ref, s0b_ref, *, S):
    QB, R, M = Q_BLOCK, GROUP, GROUP * Q_BLOCK
    NC = S // CMP_STRIDE
    n_sel = S // SEL_BLOCK
    top_n = min(SEL_TOP_N, n_sel)
    WK = WINDOW + QB
    q0 = pl.program_id(1) * QB
    s0_refs = (s0a_ref, s0b_ref)

    def heads(a):
        return jnp.concatenate([a] * R, axis=1)

    ce = lax.broadcasted_iota(jnp.int32, (NC, QB), 0) * CMP_STRIDE + (CMP_BLOCK - 1)
    vis = (q0 + lax.broadcasted_iota(jnp.int32, (NC, QB), 1)) >= ce
    vbias = heads(jnp.where(vis, 0.0, NEG))
    any_vis = heads((q0 + lax.broadcasted_iota(jnp.int32, (1, QB), 1)) >= CMP_BLOCK - 1)

    jrow = lax.broadcasted_iota(jnp.int32, (SEL_BLOCK, QB), 0)
    cur = (q0 + lax.broadcasted_iota(jnp.int32, (SEL_BLOCK, QB), 1)) // SEL_BLOCK
    future = jrow > cur
    forced = (jrow == 0) | (jrow == cur) | (jrow == cur - 1)
    sub8 = lax.broadcasted_iota(jnp.int32, (SUBLANES, QB), 0)

    ws = pl.multiple_of(jnp.maximum(q0 - WINDOW, 0), QB)
    dw = (q0 + lax.broadcasted_iota(jnp.int32, (WK, QB), 1)) - (ws + lax.broadcasted_iota(jnp.int32, (WK, QB), 0))
    wbias = heads(jnp.where(dw >= 0, jnp.where(dw < WINDOW, 0.0, NEG), NEG))

    qas = [jnp.concatenate([qa_ref[:, (g * R + r) * LANES:(g * R + r + 1) * LANES] for r in range(R)],
                           axis=0) for g in range(N_KV)]
    s_cs = [_dot_nt(kca_ref[0, g], qas[g]) for g in range(N_KV)]
    s_ws = [_dot_nt(kwa_ref[g, pl.ds(ws, WK), :], qas[g]) for g in range(N_KV)]

    o_cs, imps = [], []
    for g in range(N_KV):
        s = s_cs[g] + vbias
        m = jnp.max(s, axis=0, keepdims=True)
        e = jnp.exp(s - m)
        l = jnp.sum(e, axis=0, keepdims=True)
        p = e * jnp.where(any_vis, 1.0 / l, 0.0)
        o_cs.append(_dot(vct_ref[0, g], p.astype(BF16)))
        psum = p[:, 0:QB] + p[:, QB:2 * QB] + p[:, 2 * QB:3 * QB] + p[:, 3 * QB:4 * QB]
        hi = psum.astype(BF16)
        r1 = psum - hi.astype(F32)
        mid = r1.astype(BF16)
        lo = (r1 - mid.astype(F32)).astype(BF16)
        ov = ov_ref[...]
        imps.append(_dot(ov, hi) + _dot(ov, mid) + _dot(ov, lo))

    def select(g):
        v = jnp.where(future, NEG, jnp.where(forced, FORCE, imps[g]))
        vt = [v[k * SUBLANES:(k + 1) * SUBLANES] for k in range(SEL_BLOCK // SUBLANES)]
        cnt = [jnp.zeros((SUBLANES, QB), jnp.int32) for _ in vt]
        for i in range(n_sel):
            vi = v[i:i + 1, :]
            for k, vk in enumerate(vt):
                if (k + 1) * SUBLANES <= i + 1:
                    beats = jnp.where(vi > vk, 1, 0)
                elif k * SUBLANES > i:
                    beats = jnp.where(vi >= vk, 1, 0)
                else:
                    beats = jnp.where(sub8 > i - k * SUBLANES, jnp.where(vi >= vk, 1, 0),
                                      jnp.where(vi > vk, 1, 0))
                cnt[k] = cnt[k] + beats
        cnt = jnp.concatenate(cnt, axis=0)
        selb = jnp.where(cnt < top_n, jnp.where(v > NEG / 2, 0.0, UNSEL), UNSEL)
        aug = jnp.concatenate([jnp.zeros_like(selb), selb], axis=0).T
        for r in range(R):
            qh = qas[g][r * QB:(r + 1) * QB].astype(F32)
            qsel_ref[g, r * QB:(r + 1) * QB, :] = (
                qh + aug + ssl_ref[g * R + r:g * R + r + 1, :]).astype(BF16)

    def window(g):
        sw = s_ws[g] + wbias
        pw = jnp.exp((sw - jnp.max(sw, axis=0, keepdims=True)).astype(BF16))
        accw = _dot(vwt_ref[g, :, pl.ds(ws, WK)], pw)
        return accw * (1.0 / accw[HEAD_DIM:HEAD_DIM + 1, :])

    def sel_scores(g, k0):
        return _dot_nt(ksa_ref[g, pl.ds(k0, SEL_CHUNK), :], qsel_ref[g])

    select(0)
    s0_refs[0][...] = sel_scores(0, 0)
    o_ws = [window(0)]
    select(1)
    o_ws.append(window(1))

    m_ref[...] = jnp.full(m_ref.shape, NEG, F32)
    acc_ref[...] = jnp.zeros(acc_ref.shape, F32)
    pp_ref[...] = jnp.zeros(pp_ref.shape, BF16)
    pa_ref[...] = jnp.ones(pa_ref.shape, F32)

    def online(g, sc):
        m_prev = m_ref[g]
        m_new = jnp.maximum(m_prev, jnp.max(sc, axis=0, keepdims=True))
        m_ref[g] = m_new
        return jnp.exp((sc - m_new).astype(BF16)), jnp.exp(m_prev - m_new)

    def pending_pv(k0):
        acc_ref[1] = pa_ref[...] * acc_ref[1] + _dot(vst_ref[1, :, pl.ds(k0, SEL_CHUNK)], pp_ref[...])

    def step(c, src_ref, dst_ref):
        last = dst_ref is None
        k0 = pl.multiple_of(c * SEL_CHUNK, SEL_CHUNK)
        kprev = pl.multiple_of(jnp.maximum(c - 1, 0) * SEL_CHUNK, SEL_CHUNK)
        s1 = sel_scores(1, k0)
        pending_pv(kprev)
        if not last:
            dst_ref[...] = sel_scores(0, pl.multiple_of(k0 + SEL_CHUNK, SEL_CHUNK))
        s0 = src_ref[...]
        if last:
            kpos = k0 + lax.broadcasted_iota(jnp.int32, (SEL_CHUNK, QB), 0)
            tq = q0 + lax.broadcasted_iota(jnp.int32, (SEL_CHUNK, QB), 1)
            cb = heads(jnp.where(kpos <= tq, 0.0, NEG))
            s0 = s0 + cb
            s1 = s1 + cb
        p0, alpha0 = online(0, s0)
        acc_ref[0] = alpha0 * acc_ref[0] + _dot(vst_ref[0, :, pl.ds(k0, SEL_CHUNK)], p0)
        p1, alpha1 = online(1, s1)
        pp_ref[...] = p1
        pa_ref[...] = alpha1

    def body(c, carry):
        for par in range(2):
            pl.when(c % 2 == par)(functools.partial(step, c, s0_refs[par], s0_refs[1 - par]))
        return carry

    nfull = q0 // SEL_CHUNK
    lax.fori_loop(0, nfull, body, 0)
    for par in range(2):
        pl.when(nfull % 2 == par)(functools.partial(step, nfull, s0_refs[par], None))
    pending_pv(pl.multiple_of(nfull * SEL_CHUNK, SEL_CHUNK))

    gt = gate_ref[...].T
    outs = []
    for g in range(N_KV):
        acc = acc_ref[g]
        o_s = acc * (1.0 / acc[HEAD_DIM:HEAD_DIM + 1, :])
        o_c, o_w = o_cs[g], o_ws[g]
        for r in range(R):
            col = (g * R + r) * 3
            cols = slice(r * QB, (r + 1) * QB)
            o = (gt[col:col + 1] * o_c[:HEAD_DIM, cols] + gt[col + 1:col + 2] * o_s[:HEAD_DIM, cols]
                 + gt[col + 2:col + 3] * o_w[:HEAD_DIM, cols])
            outs.append(o)
    o_ref[...] = jnp.concatenate(outs, axis=0).T.astype(BF16)


def _nsa(qa, gates, kca, vct, ksa, vst, kwa, vwt, ov, ssl, B, S):
    NC = S // CMP_STRIDE
    nqb = S // Q_BLOCK
    M = GROUP * Q_BLOCK
    tok = lambda w_: pl.BlockSpec((Q_BLOCK, w_), lambda b, i: (b * nqb + i, 0))
    const = lambda shape: pl.BlockSpec(shape, lambda b, i: (0,) * len(shape))
    seq = pl.BlockSpec((N_KV, S, LANES), lambda b, i: (0, b, 0))
    seq_t = pl.BlockSpec((N_KV, LANES, S), lambda b, i: (0, 0, b))
    return pl.pallas_call(
        functools.partial(_nsa_kernel, S=S),
        grid=(B, nqb),
        in_specs=[tok(N_HEADS * LANES), tok(LANES),
                  pl.BlockSpec((1, N_KV, NC, LANES), lambda b, i: (b, 0, 0, 0)),
                  pl.BlockSpec((1, N_KV, LANES, NC), lambda b, i: (b, 0, 0, 0)),
                  seq, seq_t, seq, seq_t,
                  const((SEL_BLOCK, NC)), const((N_HEADS, LANES))],
        out_specs=tok(ATTN_W),
        out_shape=jax.ShapeDtypeStruct((B * S, ATTN_W), BF16),
        scratch_shapes=[pltpu.VMEM((N_KV, M, LANES), BF16), pltpu.VMEM((N_KV, 1, M), F32),
                        pltpu.VMEM((N_KV, LANES, M), F32),
                        pltpu.VMEM((SEL_CHUNK, M), BF16), pltpu.VMEM((1, M), F32),
                        pltpu.VMEM((SEL_CHUNK, M), F32), pltpu.VMEM((SEL_CHUNK, M), F32)],
        compiler_params=_params(("parallel", "arbitrary")),
    )(qa, gates, kca, vct, ksa, vst, kwa, vwt, ov, ssl)


def _mix_kernel(x_ref, attn_ref, u_ref, uh_ref, g1_ref, wm_ref, wap_ref, wpool_ref, ps_ref,
                wpp_ref, wout_ref, o_ref, *, nst):
    TM = x_ref.shape[0]
    seq_tile = pl.program_id(0) % nst
    x = x_ref[...]
    h = _rmsnorm(x, g1_ref[...]).astype(BF16)
    merge = jax.nn.sigmoid(_dot(h, wm_ref[...]))
    a = _dot(attn_ref[...], wap_ref[...])

    halo = jnp.where(seq_tile == 0, 0.0, uh_ref[...])
    ue = jnp.concatenate([halo, u_ref[...]], axis=0)
    pos = seq_tile * TM + lax.broadcasted_iota(jnp.int32, (TM, 1), 0)
    ys = []
    for gi, w in enumerate(POOL_WINDOWS):
        ug = ue[:, gi * POOL_DIM:(gi + 1) * POOL_DIM]
        s = ug
        sh = 1
        while sh < w:
            s = s + pltpu.roll(s, sh, 0)
            sh *= 2
        cnt = jnp.minimum(pos + 1, w).astype(F32)
        p = s[POOL_HALO:] / cnt - ug[POOL_HALO:]
        ys.append(_dot(p.astype(BF16), wpool_ref[gi]))
    y = jnp.concatenate(ys, axis=1) * ps_ref[...]
    b = _dot(y.astype(BF16), wpp_ref[...])
    mix = merge[:, :D_MODEL] * a + merge[:, D_MODEL:] * b
    o_ref[...] = x + _dot(mix.astype(BF16), wout_ref[...])


def _mix(x2, attn, u, g1, wm, wap, wpool, ps, wpp, wout, S):
    T = x2.shape[0]
    TM = TOK_TILE
    nst = S // TM
    tok = lambda w_: pl.BlockSpec((TM, w_), lambda i: (i, 0))
    const = lambda shape: pl.BlockSpec(shape, lambda i: (0,) * len(shape))
    halo = pl.BlockSpec((POOL_HALO, POOL_W), lambda i: (jnp.maximum(i * (TM // POOL_HALO) - 1, 0), 0))
    return pl.pallas_call(
        functools.partial(_mix_kernel, nst=nst),
        grid=(T // TM,),
        in_specs=[tok(D_MODEL), tok(ATTN_W), tok(POOL_W), halo, const((1, D_MODEL)),
                  const((D_MODEL, 2 * D_MODEL)), const((ATTN_W, D_MODEL)),
                  const((POOL_GROUPS, POOL_DIM, POOL_DIM)), const((1, POOL_W)),
                  const((POOL_W, D_MODEL)), const((D_MODEL, D_MODEL))],
        out_specs=tok(D_MODEL),
        out_shape=jax.ShapeDtypeStruct((T, D_MODEL), F32),
        compiler_params=_params(("parallel",)),
    )(x2, attn, u, u, g1, wm, wap, wpool, ps, wpp, wout)


def _ffn_kernel(x_ref, g2_ref, w1_ref, w2_ref, gf_ref, o_ref, *, final):
    x = x_ref[...]
    h = _rmsnorm(x, g2_ref[...]).astype(BF16)
    acc = x
    for c in range(D_FF // FF_CHUNK):
        cols = slice(c * FF_CHUNK, (c + 1) * FF_CHUNK)
        t = jnp.square(jnp.maximum(_dot(h, w1_ref[:, cols]), 0.0)).astype(BF16)
        acc = acc + _dot(t, w2_ref[cols, :])
    if final:
        acc = _rmsnorm(acc, gf_ref[...])
    o_ref[...] = acc


def _ffn(x2, g2, w1, w2, gf, final):
    T = x2.shape[0]
    TM = TOK_TILE
    tok = pl.BlockSpec((TM, D_MODEL), lambda i: (i, 0))
    const = lambda shape: pl.BlockSpec(shape, lambda i: (0,) * len(shape))
    return pl.pallas_call(
        functools.partial(_ffn_kernel, final=final),
        grid=(T // TM,),
        in_specs=[tok, const((1, D_MODEL)), const((D_MODEL, D_FF)), const((D_FF, D_MODEL)),
                  const((1, D_MODEL))],
        out_specs=tok,
        out_shape=jax.ShapeDtypeStruct((T, D_MODEL), F32),
        compiler_params=_params(("parallel",)),
    )(x2, g2, w1, w2, gf)


def _pack_w_in(w_in_l):
    scale = HEAD_DIM ** -0.5
    wq = w_in_l[:, :ATTN_W] * scale
    o_kv = ATTN_W
    o_gate = o_kv + 3 * KV_W
    o_pool = o_gate + 3 * N_HEADS
    o_merge = o_pool + POOL_W
    wgate = jnp.pad(w_in_l[:, o_gate:o_pool], ((0, 0), (0, LANES - 3 * N_HEADS)))
    w = jnp.concatenate([wq, w_in_l[:, o_kv:o_gate], wgate, w_in_l[:, o_pool:o_merge]], axis=1)
    return w.astype(BF16), w_in_l[:, o_merge:].astype(BF16)


def _pack_compress(pe_l, w1_l, w2_l):
    eye = jnp.eye(N_KV, dtype=F32)
    half = CMP_BLOCK // 2
    w1r = w1_l.reshape(2, 2, half, HEAD_DIM, CMP_HIDDEN)
    w1big = jnp.einsum('khldm,gG->hlkgdGm', w1r, eye)
    w1big = w1big.reshape(2, half, 2, N_KV * HEAD_DIM, N_KV * CMP_HIDDEN)
    per = pe_l.reshape(2, 2, half, HEAD_DIM)
    pebig = jnp.broadcast_to(per.transpose(1, 2, 0, 3)[:, :, :, None, :],
                             (2, half, 2, N_KV, HEAD_DIM)).reshape(2, half, 2, N_KV * HEAD_DIM)
    w2p = jnp.pad(w2_l, ((0, 0), (0, 0), (0, LANES - HEAD_DIM)))
    w2big = jnp.einsum('kme,kK,gG->kgmKGe', w2p, eye, eye)
    w2big = w2big.reshape(2 * N_KV * CMP_HIDDEN, 2 * N_KV * LANES)
    return pebig, w1big.astype(BF16), w2big.astype(BF16)


def _constants(S):
    NC = S // CMP_STRIDE
    n_sel = S // SEL_BLOCK
    pos = np.arange(S)
    ksc = np.zeros((S, LANES), np.float32)
    ksc[:, HEAD_DIM] = pos % SEL_BLOCK
    blk = pos // SEL_BLOCK
    sel_rows = blk >= 1
    ksc[pos[sel_rows], HEAD_DIM + blk[sel_rows]] = 1.0
    kwc = np.zeros((S, LANES), np.float32)
    kwc[:, HEAD_DIM] = pos % SEL_BLOCK
    kwc[:, HEAD_DIM + 1] = blk
    vc = np.zeros((1, LANES), np.float32)
    vc[0, HEAD_DIM] = 1.0
    qc = np.zeros((1, N_HEADS * LANES), np.float32)
    ssl = np.zeros((N_HEADS, LANES), np.float32)
    for h in range(N_HEADS):
        qc[0, h * LANES + HEAD_DIM] = SLOPES[h]
        qc[0, h * LANES + HEAD_DIM + 1] = SLOPES[h] * SEL_BLOCK
        ssl[h, HEAD_DIM + 2:] = SLOPES[h] * SEL_BLOCK * np.arange(2, SEL_BLOCK)
    c = np.arange(NC)
    cc = np.zeros((NC, LANES), np.float32)
    cstart = c * CMP_STRIDE
    cc[:, HEAD_DIM] = cstart % SEL_BLOCK
    cc[:, HEAD_DIM + 1] = cstart // SEL_BLOCK
    j = np.arange(SEL_BLOCK)
    ovt = ((cstart[None, :] <= j[:, None] * SEL_BLOCK + SEL_BLOCK - 1)
           & (cstart[None, :] + CMP_BLOCK - 1 >= j[:, None] * SEL_BLOCK)
           & (j[:, None] < n_sel) & (c[None, :] < NC - 1)).astype(np.float32)
    return (jnp.asarray(qc), jnp.asarray(ksc), jnp.asarray(kwc), jnp.asarray(vc), jnp.asarray(cc),
            jnp.asarray(ovt, BF16), jnp.asarray(ssl))


def kernel(x, norm1_g, w_in, cmp_pe, cmp_w1, cmp_w2, w_attn_proj, w_pool, pool_scale,
           w_pool_proj, w_out, norm2_g, w_ff1, w_ff2, final_g):
    B, S, _ = x.shape
    depth = w_in.shape[0]
    assert S % SEL_CHUNK == 0 and S % TOK_TILE == 0 and S // SEL_BLOCK <= SEL_BLOCK and S >= WINDOW + Q_BLOCK
    qc, ksc, kwc, vc, cc, ovt, ssl = _constants(S)
    x2 = x.reshape(B * S, D_MODEL)
    for l in range(depth):
        w_pack, w_merge = _pack_w_in(w_in[l])
        pebig, w1big, w2big = _pack_compress(cmp_pe[l], cmp_w1[l], cmp_w2[l])
        g1 = norm1_g[l].reshape(1, D_MODEL)
        qa, kvc, ksa, vst, kwa, vwt, gates, u = _inproj(x2, g1, w_pack, qc, ksc, kwc, vc, S)
        kca, vct = _compress(kvc, pebig, w1big, w2big, cc, B)
        attn = _nsa(qa, gates, kca, vct, ksa, vst, kwa, vwt, ovt, ssl, B, S)
        x2 = _mix(x2, attn, u, g1, w_merge, w_attn_proj[l].astype(BF16), w_pool[l].astype(BF16),
                  pool_scale[l].reshape(1, POOL_W), w_pool_proj[l].astype(BF16),
                  w_out[l].astype(BF16), S)
        x2 = _ffn(x2, norm2_g[l].reshape(1, D_MODEL), w_ff1[l].astype(BF16), w_ff2[l].astype(BF16),
                  final_g.reshape(1, D_MODEL), final=(l == depth - 1))
    return x2.reshape(B, S, D_MODEL)
```

```python
import functools

import numpy as np
import jax
import jax.numpy as jnp
from jax import lax
from jax.experimental import pallas as pl
from jax.experimental.pallas import tpu as pltpu

F32 = jnp.float32
BF16 = jnp.bfloat16

D_MODEL = 1024
N_HEADS = 8
HEAD_DIM = 64
N_KV = 2
GROUP = N_HEADS // N_KV
ATTN_W = N_HEADS * HEAD_DIM
KV_W = 2 * N_KV * HEAD_DIM
CMP_BLOCK = 32
CMP_STRIDE = 16
CMP_HIDDEN = 64
SEL_BLOCK = 64
SEL_TOP_N = 16
WINDOW = 512
POOL_GROUPS = 4
POOL_DIM = 128
POOL_W = POOL_GROUPS * POOL_DIM
POOL_WINDOWS = (2, 4, 8, 16)
POOL_HALO = 16
D_FF = 4 * D_MODEL
EPS = 1e-6
NEG = -1e30
FORCE = 1e6
UNSEL = -1e9
SLOPES = tuple(2.0 ** (-8.0 * (h + 1) / N_HEADS) for h in range(N_HEADS))

LANES = 128
SUBLANES = 8
Q_BLOCK = 256
SEL_CHUNK = 512
TOK_TILE = 512
FF_CHUNK = 1024
VMEM_LIMIT = 56 * 1024 * 1024
SHIFT_SLACK = 12.0
RANGE_SLACK = 12.0

_C_Q = 0
_C_KC = _C_Q + ATTN_W
_C_KS = _C_KC + KV_W
_C_VS = _C_KS + LANES
_C_KW = _C_VS + LANES
_C_VW = _C_KW + LANES
_C_GATE = _C_VW + LANES
_C_U = _C_GATE + LANES
_C_END = _C_U + POOL_W


def _dot(a, b):
    return jnp.dot(a, b, preferred_element_type=F32)


def _dot_nt(a, b):
    return lax.dot_general(a, b, (((1,), (1,)), ((), ())), preferred_element_type=F32)


def _rmsnorm(x, g):
    return x * lax.rsqrt(jnp.mean(x * x, axis=-1, keepdims=True) + EPS) * g


def _params(sem):
    return pltpu.CompilerParams(dimension_semantics=sem, vmem_limit_bytes=VMEM_LIMIT)


def _inproj_kernel(x_ref, g_ref, w_ref, qc_ref, ksc_ref, kwc_ref, vc_ref,
                   qa_ref, kvc_ref, ksa_ref, vst_ref, kwa_ref, vwt_ref, gate_ref, u_ref):
    x = x_ref[...]
    h = _rmsnorm(x, g_ref[...]).astype(BF16)

    r_all = _dot(h, w_ref[...])

    def proj(a, b):
        return r_all[:, a:b]

    kvc_ref[0] = proj(_C_KC, _C_KC + LANES)
    kvc_ref[1] = proj(_C_KC + LANES, _C_KS)
    low = lax.broadcasted_iota(jnp.int32, (x.shape[0], LANES), 1) < HEAD_DIM

    def split(r, c0, c1):
        return jnp.where(low, r, c0), jnp.where(low, pltpu.roll(r, HEAD_DIM, 1), c1)

    rq = proj(_C_Q, _C_KC)
    for j in range(N_HEADS // 2):
        lanes = [slice((2 * j + i) * LANES, (2 * j + i + 1) * LANES) for i in range(2)]
        q0, q1 = split(rq[:, j * LANES:(j + 1) * LANES], qc_ref[:, lanes[0]], qc_ref[:, lanes[1]])
        qa_ref[:, lanes[0]] = q0.astype(BF16)
        qa_ref[:, lanes[1]] = q1.astype(BF16)
    for k_ref, v_ref, kc_ref, c0 in ((ksa_ref, vst_ref, ksc_ref, _C_KS), (kwa_ref, vwt_ref, kwc_ref, _C_KW)):
        k0, k1 = split(proj(c0, c0 + LANES), kc_ref[...], kc_ref[...])
        k_ref[0] = k0.astype(BF16)
        k_ref[1] = k1.astype(BF16)
        v0, v1 = split(proj(c0 + LANES, c0 + 2 * LANES), vc_ref[...], vc_ref[...])
        v_ref[0] = v0.T.astype(BF16)
        v_ref[1] = v1.T.astype(BF16)
    gate_ref[...] = jax.nn.sigmoid(proj(_C_GATE, _C_U))
    u_ref[...] = proj(_C_U, _C_END)


def _inproj(x2, g1, w, qc, ksc, kwc, vc, S):
    T = x2.shape[0]
    TM = TOK_TILE
    nst = S // TM
    tok = lambda w_: pl.BlockSpec((TM, w_), lambda i: (i, 0))
    const = lambda shape: pl.BlockSpec(shape, lambda i: (0,) * len(shape))
    seq = pl.BlockSpec((TM, LANES), lambda i: (i % nst, 0))
    grp = pl.BlockSpec((N_KV, TM, LANES), lambda i: (0, i, 0))
    grp_t = pl.BlockSpec((N_KV, LANES, TM), lambda i: (0, 0, i))
    return pl.pallas_call(
        _inproj_kernel,
        grid=(T // TM,),
        in_specs=[tok(D_MODEL), const((1, D_MODEL)), const((D_MODEL, _C_END)),
                  const((1, N_HEADS * LANES)), seq, seq, const((1, LANES))],
        out_specs=[tok(N_HEADS * LANES), pl.BlockSpec((2, TM, LANES), lambda i: (0, i, 0)),
                   grp, grp_t, grp, grp_t, tok(LANES), tok(POOL_W)],
        out_shape=[jax.ShapeDtypeStruct((T, N_HEADS * LANES), BF16),
                   jax.ShapeDtypeStruct((2, T, LANES), F32),
                   jax.ShapeDtypeStruct((N_KV, T, LANES), BF16),
                   jax.ShapeDtypeStruct((N_KV, LANES, T), BF16),
                   jax.ShapeDtypeStruct((N_KV, T, LANES), BF16),
                   jax.ShapeDtypeStruct((N_KV, LANES, T), BF16),
                   jax.ShapeDtypeStruct((T, LANES), F32),
                   jax.ShapeDtypeStruct((T, POOL_W), F32)],
        compiler_params=_params(("parallel",)),
    )(x2, g1, w, qc, ksc, kwc, vc)


def _compress_kernel(x_ref, pe_ref, w1_ref, w2_ref, cc_ref, kca_ref, vct_ref):
    nc = x_ref.shape[1] // CMP_STRIDE
    hs = []
    for kv in range(2):
        a = jnp.zeros((nc, LANES), F32)
        b = jnp.zeros((nc, LANES), F32)
        for l in range(CMP_STRIDE):
            xl = x_ref[kv, pl.ds(l, nc, stride=CMP_STRIDE), :]
            a = a + _dot((xl + pe_ref[0, l, kv:kv + 1, :]).astype(BF16), w1_ref[0, l, kv])
            b = b + _dot((xl + pe_ref[1, l, kv:kv + 1, :]).astype(BF16), w1_ref[1, l, kv])
        hs.append(a + pltpu.roll(b, nc - 1, 0))
    h = jnp.concatenate(hs, axis=1)
    h = jax.nn.gelu(h).astype(BF16)
    out = _dot(h, w2_ref[...])
    cc = cc_ref[...]
    for g in range(N_KV):
        kca_ref[0, g] = (out[:, g * LANES:(g + 1) * LANES] + cc).astype(BF16)
        vct_ref[0, g] = out[:, (N_KV + g) * LANES:(N_KV + g + 1) * LANES].T.astype(BF16)


def _compress(kvc3, pe, w1, w2, cc, B):
    S = kvc3.shape[1] // B
    NC = S // CMP_STRIDE
    const = lambda shape: pl.BlockSpec(shape, lambda b: (0,) * len(shape))
    return pl.pallas_call(
        _compress_kernel,
        grid=(B,),
        in_specs=[pl.BlockSpec((2, S, LANES), lambda b: (0, b, 0)), const((2, CMP_STRIDE, 2, LANES)),
                  const((2, CMP_STRIDE, 2, LANES, LANES)),
                  const((2 * N_KV * CMP_HIDDEN, 2 * N_KV * LANES)), const((NC, LANES))],
        out_specs=[pl.BlockSpec((1, N_KV, NC, LANES), lambda b: (b, 0, 0, 0)),
                   pl.BlockSpec((1, N_KV, LANES, NC), lambda b: (b, 0, 0, 0))],
        out_shape=[jax.ShapeDtypeStruct((B, N_KV, NC, LANES), BF16),
                   jax.ShapeDtypeStruct((B, N_KV, LANES, NC), BF16)],
        compiler_params=_params(("parallel",)),
    )(kvc3, pe, w1, w2, cc)


def _nsa_kernel(qa_ref, gate_ref, kca_ref, vct_ref, ksa_ref, vst_ref, kwa_ref, vwt_ref,
                ov_ref, ssl_ref, slv_ref, o_ref, qsel_ref, m_ref, acc_ref, pp_ref, pa_ref, s0a_ref, s0b_ref,
                mx_ref, sh_ref, ex_ref, *, S):
    QB, R, M = Q_BLOCK, GROUP, GROUP * Q_BLOCK
    NC = S // CMP_STRIDE
    n_sel = S // SEL_BLOCK
    top_n = min(SEL_TOP_N, n_sel)
    WK = WINDOW + QB
    q0 = pl.program_id(1) * QB
    s0_refs = (s0a_ref, s0b_ref)

    def heads(a):
        return jnp.concatenate([a] * R, axis=1)

    ce = lax.broadcasted_iota(jnp.int32, (NC, QB), 0) * CMP_STRIDE + (CMP_BLOCK - 1)
    vis = (q0 + lax.broadcasted_iota(jnp.int32, (NC, QB), 1)) >= ce
    vbias = heads(jnp.where(vis, 0.0, NEG))
    any_vis = heads((q0 + lax.broadcasted_iota(jnp.int32, (1, QB), 1)) >= CMP_BLOCK - 1)

    jrow = lax.broadcasted_iota(jnp.int32, (SEL_BLOCK, QB), 0)
    cur = (q0 + lax.broadcasted_iota(jnp.int32, (SEL_BLOCK, QB), 1)) // SEL_BLOCK
    future = jrow > cur
    forced = (jrow == 0) | (jrow == cur) | (jrow == cur - 1)
    sub8 = lax.broadcasted_iota(jnp.int32, (SUBLANES, QB), 0)

    ws = pl.multiple_of(jnp.maximum(q0 - WINDOW, 0), QB)
    dw = (q0 + lax.broadcasted_iota(jnp.int32, (WK, QB), 1)) - (ws + lax.broadcasted_iota(jnp.int32, (WK, QB), 0))
    wbias = heads(jnp.where(dw >= 0, jnp.where(dw < WINDOW, 0.0, NEG), NEG))

    qas = [jnp.concatenate([qa_ref[:, (g * R + r) * LANES:(g * R + r + 1) * LANES] for r in range(R)],
                           axis=0) for g in range(N_KV)]
    s_cs = [_dot_nt(kca_ref[0, g], qas[g]) for g in range(N_KV)]
    s_ws = [_dot_nt(kwa_ref[g, pl.ds(ws, WK), :], qas[g]) for g in range(N_KV)]

    o_cs, imps = [], []
    for g in range(N_KV):
        s = s_cs[g] + vbias
        m = jnp.max(s, axis=0, keepdims=True)
        e = jnp.exp(s - m)
        l = jnp.sum(e, axis=0, keepdims=True)
        p = e * jnp.where(any_vis, 1.0 / l, 0.0)
        o_cs.append(_dot(vct_ref[0, g], p.astype(BF16)))
        psum = p[:, 0:QB] + p[:, QB:2 * QB] + p[:, 2 * QB:3 * QB] + p[:, 3 * QB:4 * QB]
        hi = psum.astype(BF16)
        r1 = psum - hi.astype(F32)
        mid = r1.astype(BF16)
        lo = (r1 - mid.astype(F32)).astype(BF16)
        ov = ov_ref[...]
        imps.append(_dot(ov, hi) + _dot(ov, mid) + _dot(ov, lo))

    def select(g):
        v = jnp.where(future, NEG, jnp.where(forced, FORCE, imps[g]))
        vt = [v[k * SUBLANES:(k + 1) * SUBLANES] for k in range(SEL_BLOCK // SUBLANES)]
        cnt = [jnp.zeros((SUBLANES, QB), jnp.int32) for _ in vt]
        for i in range(n_sel):
            vi = v[i:i + 1, :]
            for k, vk in enumerate(vt):
                if (k + 1) * SUBLANES <= i + 1:
                    beats = jnp.where(vi > vk, 1, 0)
                elif k * SUBLANES > i:
                    beats = jnp.where(vi >= vk, 1, 0)
                else:
                    beats = jnp.where(sub8 > i - k * SUBLANES, jnp.where(vi >= vk, 1, 0),
                                      jnp.where(vi > vk, 1, 0))
                cnt[k] = cnt[k] + beats
        cnt = jnp.concatenate(cnt, axis=0)
        selb = jnp.where(cnt < top_n, jnp.where(v > NEG / 2, 0.0, UNSEL), UNSEL)
        aug = jnp.concatenate([jnp.zeros_like(selb), selb], axis=0).T
        for r in range(R):
            qh = qas[g][r * QB:(r + 1) * QB].astype(F32)
            qsel_ref[g, r * QB:(r + 1) * QB, :] = (
                qh + aug + ssl_ref[g * R + r:g * R + r + 1, :]).astype(BF16)

    def window(g):
        sw = s_ws[g] + wbias
        pw = jnp.exp((sw - jnp.max(sw, axis=0, keepdims=True)).astype(BF16))
        accw = _dot(vwt_ref[g, :, pl.ds(ws, WK)], pw)
        return accw * (1.0 / accw[HEAD_DIM:HEAD_DIM + 1, :])

    def sel_scores(g, k0):
        return _dot_nt(ksa_ref[g, pl.ds(k0, SEL_CHUNK), :], qsel_ref[g])

    select(0)
    o_ws = [window(0)]
    select(1)
    o_ws.append(window(1))

    tvec = heads(q0 + lax.broadcasted_iota(jnp.int32, (1, QB), 1)).astype(F32)

    def bias_cap(g, k_hi):
        return slv_ref[g] * jnp.minimum(tvec, k_hi)

    def run_selected(streaming):
        acc_ref[...] = jnp.zeros(acc_ref.shape, F32)
        pp_ref[...] = jnp.zeros(pp_ref.shape, BF16)
        pa_ref[...] = jnp.ones(pa_ref.shape, F32)
        if streaming:
            kpos0 = lax.broadcasted_iota(jnp.int32, (SEL_BLOCK, QB), 0)
            tq0 = q0 + lax.broadcasted_iota(jnp.int32, (SEL_BLOCK, QB), 1)
            cb0 = heads(jnp.where(kpos0 <= tq0, 0.0, NEG))
            for g in range(N_KV):
                t0 = _dot_nt(ksa_ref[g, 0:SEL_BLOCK, :], qsel_ref[g]) + cb0
                m0 = jnp.max(t0, axis=0, keepdims=True)
                r0 = m0 - bias_cap(g, SEL_BLOCK - 1.0)
                m_ref[g] = r0
                mx_ref[g] = m0
                sh_ref[g] = r0 + bias_cap(g, SEL_CHUNK - 1.0)
            ex_ref[...] = jnp.zeros(ex_ref.shape, F32)
        else:
            m_ref[...] = jnp.full(m_ref.shape, NEG, F32)

        def online(g, sc, k0):
            m_cur = jnp.max(sc, axis=0, keepdims=True)
            m_prev = m_ref[g]
            if not streaming:
                m_new = jnp.maximum(m_prev, m_cur)
                m_ref[g] = m_new
                return jnp.exp((sc - m_new).astype(BF16)), jnp.exp(m_prev - m_new)
            cap = bias_cap(g, (k0 + (SEL_CHUNK - 1)).astype(F32))
            shift = m_prev + cap
            r_cur = m_cur - cap
            m_ref[g] = jnp.maximum(m_prev, r_cur)
            mx_ref[g] = jnp.maximum(mx_ref[g], m_cur)
            ex_ref[g] = jnp.maximum(ex_ref[g], r_cur - m_prev)
            alpha = jnp.exp(sh_ref[g] - shift)
            sh_ref[g] = shift
            return jnp.exp((sc - shift).astype(BF16)), alpha

        def pending_pv(k0):
            acc_ref[1] = pa_ref[...] * acc_ref[1] + _dot(vst_ref[1, :, pl.ds(k0, SEL_CHUNK)], pp_ref[...])

        def step(c, src_ref, dst_ref):
            last = dst_ref is None
            k0 = pl.multiple_of(c * SEL_CHUNK, SEL_CHUNK)
            kprev = pl.multiple_of(jnp.maximum(c - 1, 0) * SEL_CHUNK, SEL_CHUNK)
            s1 = sel_scores(1, k0)
            pending_pv(kprev)
            if not last:
                dst_ref[...] = sel_scores(0, pl.multiple_of(k0 + SEL_CHUNK, SEL_CHUNK))
            s0 = src_ref[...]
            if last:
                kpos = k0 + lax.broadcasted_iota(jnp.int32, (SEL_CHUNK, QB), 0)
                tq = q0 + lax.broadcasted_iota(jnp.int32, (SEL_CHUNK, QB), 1)
                cb = heads(jnp.where(kpos <= tq, 0.0, NEG))
                s0 = s0 + cb
                s1 = s1 + cb
            p0, alpha0 = online(0, s0, k0)
            acc_ref[0] = alpha0 * acc_ref[0] + _dot(vst_ref[0, :, pl.ds(k0, SEL_CHUNK)], p0)
            p1, alpha1 = online(1, s1, k0)
            pp_ref[...] = p1
            pa_ref[...] = alpha1

        def body(c, carry):
            for par in range(2):
                pl.when(c % 2 == par)(functools.partial(step, c, s0_refs[par], s0_refs[1 - par]))
            return carry

        s0_refs[0][...] = sel_scores(0, 0)
        nfull = q0 // SEL_CHUNK
        lax.fori_loop(0, nfull, body, 0)
        for par in range(2):
            pl.when(nfull % 2 == par)(functools.partial(step, nfull, s0_refs[par], None))
        pending_pv(pl.multiple_of(nfull * SEL_CHUNK, SEL_CHUNK))

    run_selected(True)
    unsafe = jnp.maximum(ex_ref[...] - SHIFT_SLACK, sh_ref[...] - mx_ref[...] - RANGE_SLACK)
    pl.when(jnp.max(unsafe) > 0.0)(functools.partial(run_selected, False))


    gt = gate_ref[...].T
    outs = []
    for g in range(N_KV):
        acc = acc_ref[g]
        o_s = acc * (1.0 / acc[HEAD_DIM:HEAD_DIM + 1, :])
        o_c, o_w = o_cs[g], o_ws[g]
        for r in range(R):
            col = (g * R + r) * 3
            cols = slice(r * QB, (r + 1) * QB)
            o = (gt[col:col + 1] * o_c[:HEAD_DIM, cols] + gt[col + 1:col + 2] * o_s[:HEAD_DIM, cols]
                 + gt[col + 2:col + 3] * o_w[:HEAD_DIM, cols])
            outs.append(o)
    o_ref[...] = jnp.concatenate(outs, axis=0).T.astype(BF16)


def _nsa(qa, gates, kca, vct, ksa, vst, kwa, vwt, ov, ssl, slv, B, S):
    NC = S // CMP_STRIDE
    nqb = S // Q_BLOCK
    M = GROUP * Q_BLOCK
    tok = lambda w_: pl.BlockSpec((Q_BLOCK, w_), lambda b, i: (b * nqb + i, 0))
    const = lambda shape: pl.BlockSpec(shape, lambda b, i: (0,) * len(shape))
    seq = pl.BlockSpec((N_KV, S, LANES), lambda b, i: (0, b, 0))
    seq_t = pl.BlockSpec((N_KV, LANES, S), lambda b, i: (0, 0, b))
    return pl.pallas_call(
        functools.partial(_nsa_kernel, S=S),
        grid=(B, nqb),
        in_specs=[tok(N_HEADS * LANES), tok(LANES),
                  pl.BlockSpec((1, N_KV, NC, LANES), lambda b, i: (b, 0, 0, 0)),
                  pl.BlockSpec((1, N_KV, LANES, NC), lambda b, i: (b, 0, 0, 0)),
                  seq, seq_t, seq, seq_t,
                  const((SEL_BLOCK, NC)), const((N_HEADS, LANES)), const((N_KV, 1, M))],
        out_specs=tok(ATTN_W),
        out_shape=jax.ShapeDtypeStruct((B * S, ATTN_W), BF16),
        scratch_shapes=[pltpu.VMEM((N_KV, M, LANES), BF16), pltpu.VMEM((N_KV, 1, M), F32),
                        pltpu.VMEM((N_KV, LANES, M), F32),
                        pltpu.VMEM((SEL_CHUNK, M), BF16), pltpu.VMEM((1, M), F32),
                        pltpu.VMEM((SEL_CHUNK, M), F32), pltpu.VMEM((SEL_CHUNK, M), F32)]
                       + [pltpu.VMEM((N_KV, 1, M), F32)] * 3,
        compiler_params=_params(("parallel", "arbitrary")),
    )(qa, gates, kca, vct, ksa, vst, kwa, vwt, ov, ssl, slv)


def _mix_kernel(x_ref, attn_ref, u_ref, uh_ref, g1_ref, wm_ref, wap_ref, wpool_ref, ps_ref,
                wpp_ref, wout_ref, o_ref, *, nst):
    TM = x_ref.shape[0]
    seq_tile = pl.program_id(0) % nst
    x = x_ref[...]
    h = _rmsnorm(x, g1_ref[...]).astype(BF16)
    merge = jax.nn.sigmoid(_dot(h, wm_ref[...]))
    a = _dot(attn_ref[...], wap_ref[...])

    halo = jnp.where(seq_tile == 0, 0.0, uh_ref[...])
    ue = jnp.concatenate([halo, u_ref[...]], axis=0)
    pos = seq_tile * TM + lax.broadcasted_iota(jnp.int32, (TM, 1), 0)
    ys = []
    for gi, w in enumerate(POOL_WINDOWS):
        ug = ue[:, gi * POOL_DIM:(gi + 1) * POOL_DIM]
        s = ug
        sh = 1
        while sh < w:
            s = s + pltpu.roll(s, sh, 0)
            sh *= 2
        cnt = jnp.minimum(pos + 1, w).astype(F32)
        p = s[POOL_HALO:] / cnt - ug[POOL_HALO:]
        ys.append(_dot(p.astype(BF16), wpool_ref[gi]))
    y = jnp.concatenate(ys, axis=1) * ps_ref[...]
    b = _dot(y.astype(BF16), wpp_ref[...])
    mix = merge[:, :D_MODEL] * a + merge[:, D_MODEL:] * b
    o_ref[...] = x + _dot(mix.astype(BF16), wout_ref[...])


def _mix(x2, attn, u, g1, wm, wap, wpool, ps, wpp, wout, S):
    T = x2.shape[0]
    TM = TOK_TILE
    nst = S // TM
    tok = lambda w_: pl.BlockSpec((TM, w_), lambda i: (i, 0))
    const = lambda shape: pl.BlockSpec(shape, lambda i: (0,) * len(shape))
    halo = pl.BlockSpec((POOL_HALO, POOL_W), lambda i: (jnp.maximum(i * (TM // POOL_HALO) - 1, 0), 0))
    return pl.pallas_call(
        functools.partial(_mix_kernel, nst=nst),
        grid=(T // TM,),
        in_specs=[tok(D_MODEL), tok(ATTN_W), tok(POOL_W), halo, const((1, D_MODEL)),
                  const((D_MODEL, 2 * D_MODEL)), const((ATTN_W, D_MODEL)),
                  const((POOL_GROUPS, POOL_DIM, POOL_DIM)), const((1, POOL_W)),
                  const((POOL_W, D_MODEL)), const((D_MODEL, D_MODEL))],
        out_specs=tok(D_MODEL),
        out_shape=jax.ShapeDtypeStruct((T, D_MODEL), F32),
        compiler_params=_params(("parallel",)),
    )(x2, attn, u, u, g1, wm, wap, wpool, ps, wpp, wout)


def _ffn_kernel(x_ref, g2_ref, w1_ref, w2_ref, gf_ref, o_ref, *, final):
    x = x_ref[...]
    h = _rmsnorm(x, g2_ref[...]).astype(BF16)
    acc = x
    for c in range(D_FF // FF_CHUNK):
        cols = slice(c * FF_CHUNK, (c + 1) * FF_CHUNK)
        t = jnp.square(jnp.maximum(_dot(h, w1_ref[:, cols]), 0.0)).astype(BF16)
        acc = acc + _dot(t, w2_ref[cols, :])
    if final:
        acc = _rmsnorm(acc, gf_ref[...])
    o_ref[...] = acc


def _ffn(x2, g2, w1, w2, gf, final):
    T = x2.shape[0]
    TM = TOK_TILE
    tok = pl.BlockSpec((TM, D_MODEL), lambda i: (i, 0))
    const = lambda shape: pl.BlockSpec(shape, lambda i: (0,) * len(shape))
    return pl.pallas_call(
        functools.partial(_ffn_kernel, final=final),
        grid=(T // TM,),
        in_specs=[tok, const((1, D_MODEL)), const((D_MODEL, D_FF)), const((D_FF, D_MODEL)),
                  const((1, D_MODEL))],
        out_specs=tok,
        out_shape=jax.ShapeDtypeStruct((T, D_MODEL), F32),
        compiler_params=_params(("parallel",)),
    )(x2, g2, w1, w2, gf)


def _pack_w_in(w_in_l):
    scale = HEAD_DIM ** -0.5
    wq = w_in_l[:, :ATTN_W] * scale
    o_kv = ATTN_W
    o_gate = o_kv + 3 * KV_W
    o_pool = o_gate + 3 * N_HEADS
    o_merge = o_pool + POOL_W
    wgate = jnp.pad(w_in_l[:, o_gate:o_pool], ((0, 0), (0, LANES - 3 * N_HEADS)))
    w = jnp.concatenate([wq, w_in_l[:, o_kv:o_gate], wgate, w_in_l[:, o_pool:o_merge]], axis=1)
    return w.astype(BF16), w_in_l[:, o_merge:].astype(BF16)


def _pack_compress(pe_l, w1_l, w2_l):
    eye = jnp.eye(N_KV, dtype=F32)
    half = CMP_BLOCK // 2
    w1r = w1_l.reshape(2, 2, half, HEAD_DIM, CMP_HIDDEN)
    w1big = jnp.einsum('khldm,gG->hlkgdGm', w1r, eye)
    w1big = w1big.reshape(2, half, 2, N_KV * HEAD_DIM, N_KV * CMP_HIDDEN)
    per = pe_l.reshape(2, 2, half, HEAD_DIM)
    pebig = jnp.broadcast_to(per.transpose(1, 2, 0, 3)[:, :, :, None, :],
                             (2, half, 2, N_KV, HEAD_DIM)).reshape(2, half, 2, N_KV * HEAD_DIM)
    w2p = jnp.pad(w2_l, ((0, 0), (0, 0), (0, LANES - HEAD_DIM)))
    w2big = jnp.einsum('kme,kK,gG->kgmKGe', w2p, eye, eye)
    w2big = w2big.reshape(2 * N_KV * CMP_HIDDEN, 2 * N_KV * LANES)
    return pebig, w1big.astype(BF16), w2big.astype(BF16)


def _constants(S):
    NC = S // CMP_STRIDE
    n_sel = S // SEL_BLOCK
    pos = np.arange(S)
    ksc = np.zeros((S, LANES), np.float32)
    ksc[:, HEAD_DIM] = pos % SEL_BLOCK
    blk = pos // SEL_BLOCK
    sel_rows = blk >= 1
    ksc[pos[sel_rows], HEAD_DIM + blk[sel_rows]] = 1.0
    kwc = np.zeros((S, LANES), np.float32)
    kwc[:, HEAD_DIM] = pos % SEL_BLOCK
    kwc[:, HEAD_DIM + 1] = blk
    vc = np.zeros((1, LANES), np.float32)
    vc[0, HEAD_DIM] = 1.0
    qc = np.zeros((1, N_HEADS * LANES), np.float32)
    ssl = np.zeros((N_HEADS, LANES), np.float32)
    for h in range(N_HEADS):
        qc[0, h * LANES + HEAD_DIM] = SLOPES[h]
        qc[0, h * LANES + HEAD_DIM + 1] = SLOPES[h] * SEL_BLOCK
        ssl[h, HEAD_DIM + 2:] = SLOPES[h] * SEL_BLOCK * np.arange(2, SEL_BLOCK)
    c = np.arange(NC)
    cc = np.zeros((NC, LANES), np.float32)
    cstart = c * CMP_STRIDE
    cc[:, HEAD_DIM] = cstart % SEL_BLOCK
    cc[:, HEAD_DIM + 1] = cstart // SEL_BLOCK
    slv = np.repeat(np.asarray(SLOPES, np.float32).reshape(N_KV, 1, GROUP), Q_BLOCK, axis=2)
    j = np.arange(SEL_BLOCK)
    ovt = ((cstart[None, :] <= j[:, None] * SEL_BLOCK + SEL_BLOCK - 1)
           & (cstart[None, :] + CMP_BLOCK - 1 >= j[:, None] * SEL_BLOCK)
           & (j[:, None] < n_sel) & (c[None, :] < NC - 1)).astype(np.float32)
    return (jnp.asarray(qc), jnp.asarray(ksc), jnp.asarray(kwc), jnp.asarray(vc), jnp.asarray(cc),
            jnp.asarray(ovt, BF16), jnp.asarray(ssl), jnp.asarray(slv))


def kernel(x, norm1_g, w_in, cmp_pe, cmp_w1, cmp_w2, w_attn_proj, w_pool, pool_scale,
           w_pool_proj, w_out, norm2_g, w_ff1, w_ff2, final_g):
    B, S, _ = x.shape
    depth = w_in.shape[0]
    assert S % SEL_CHUNK == 0 and S % TOK_TILE == 0 and S // SEL_BLOCK <= SEL_BLOCK and S >= WINDOW + Q_BLOCK
    qc, ksc, kwc, vc, cc, ovt, ssl, slv = _constants(S)
    x2 = x.reshape(B * S, D_MODEL)
    for l in range(depth):
        w_pack, w_merge = _pack_w_in(w_in[l])
        pebig, w1big, w2big = _pack_compress(cmp_pe[l], cmp_w1[l], cmp_w2[l])
        g1 = norm1_g[l].reshape(1, D_MODEL)
        qa, kvc, ksa, vst, kwa, vwt, gates, u = _inproj(x2, g1, w_pack, qc, ksc, kwc, vc, S)
        kca, vct = _compress(kvc, pebig, w1big, w2big, cc, B)
        attn = _nsa(qa, gates, kca, vct, ksa, vst, kwa, vwt, ovt, ssl, slv, B, S)
        x2 = _mix(x2, attn, u, g1, w_merge, w_attn_proj[l].astype(BF16), w_pool[l].astype(BF16),
                  pool_scale[l].reshape(1, POOL_W), w_pool_proj[l].astype(BF16),
                  w_out[l].astype(BF16), S)
        x2 = _ffn(x2, norm2_g[l].reshape(1, D_MODEL), w_ff1[l].astype(BF16), w_ff2[l].astype(BF16),
                  final_g.reshape(1, D_MODEL), final=(l == depth - 1))
    return x2.reshape(B, S, D_MODEL)
```

```python
import functools

import numpy as np
import jax
import jax.numpy as jnp
from jax import lax
from jax.experimental import pallas as pl
from jax.experimental.pallas import tpu as pltpu

F32 = jnp.float32
BF16 = jnp.bfloat16

D_MODEL = 1024
N_HEADS = 8
HEAD_DIM = 64
N_KV = 2
GROUP = N_HEADS // N_KV
ATTN_W = N_HEADS * HEAD_DIM
KV_W = 2 * N_KV * HEAD_DIM
CMP_BLOCK = 32
CMP_STRIDE = 16
CMP_HIDDEN = 64
SEL_BLOCK = 64
SEL_TOP_N = 16
WINDOW = 512
POOL_GROUPS = 4
POOL_DIM = 128
POOL_W = POOL_GROUPS * POOL_DIM
POOL_WINDOWS = (2, 4, 8, 16)
POOL_HALO = 16
D_FF = 4 * D_MODEL
EPS = 1e-6
NEG = -1e30
FORCE = 1e6
UNSEL = -1e9
SLOPES = tuple(2.0 ** (-8.0 * (h + 1) / N_HEADS) for h in range(N_HEADS))

LANES = 128
SUBLANES = 8
Q_BLOCK = 256
SEL_CHUNK = 512
TOK_TILE = 1024
FF_CHUNK = 1024
VMEM_LIMIT = 56 * 1024 * 1024
SHIFT_SLACK = 12.0
RANGE_SLACK = 12.0

_C_Q = 0
_C_KC = _C_Q + ATTN_W
_C_KS = _C_KC + KV_W
_C_VS = _C_KS + LANES
_C_KW = _C_VS + LANES
_C_VW = _C_KW + LANES
_C_GATE = _C_VW + LANES
_C_U = _C_GATE + LANES
_C_END = _C_U + POOL_W


def _dot(a, b):
    return jnp.dot(a, b, preferred_element_type=F32)


def _dot_nt(a, b):
    return lax.dot_general(a, b, (((1,), (1,)), ((), ())), preferred_element_type=F32)


def _rmsnorm(x, g):
    return x * lax.rsqrt(jnp.mean(x * x, axis=-1, keepdims=True) + EPS) * g


def _const_spec(shape):
    return pl.BlockSpec(shape, lambda i: (0,) * len(shape), pipeline_mode=pl.Buffered(1))


def _params(sem):
    return pltpu.CompilerParams(dimension_semantics=sem, vmem_limit_bytes=VMEM_LIMIT)


def _inproj_kernel(x_ref, g_ref, w_ref, qc_ref, ksc_ref, kwc_ref, vc_ref,
                   qa_ref, kvc_ref, ksa_ref, vst_ref, kwa_ref, vwt_ref, gate_ref, u_ref):
    x = x_ref[...]
    h = _rmsnorm(x, g_ref[...]).astype(BF16)

    r_all = _dot(h, w_ref[...])

    def proj(a, b):
        return r_all[:, a:b]

    kvc_ref[0] = proj(_C_KC, _C_KC + LANES)
    kvc_ref[1] = proj(_C_KC + LANES, _C_KS)
    low = lax.broadcasted_iota(jnp.int32, (x.shape[0], LANES), 1) < HEAD_DIM

    def split(r, c0, c1):
        return jnp.where(low, r, c0), jnp.where(low, pltpu.roll(r, HEAD_DIM, 1), c1)

    rq = proj(_C_Q, _C_KC)
    for j in range(N_HEADS // 2):
        lanes = [slice((2 * j + i) * LANES, (2 * j + i + 1) * LANES) for i in range(2)]
        q0, q1 = split(rq[:, j * LANES:(j + 1) * LANES], qc_ref[:, lanes[0]], qc_ref[:, lanes[1]])
        qa_ref[:, lanes[0]] = q0.astype(BF16)
        qa_ref[:, lanes[1]] = q1.astype(BF16)
    for k_ref, v_ref, kc_ref, c0 in ((ksa_ref, vst_ref, ksc_ref, _C_KS), (kwa_ref, vwt_ref, kwc_ref, _C_KW)):
        k0, k1 = split(proj(c0, c0 + LANES), kc_ref[...], kc_ref[...])
        k_ref[0] = k0.astype(BF16)
        k_ref[1] = k1.astype(BF16)
        v0, v1 = split(proj(c0 + LANES, c0 + 2 * LANES), vc_ref[...], vc_ref[...])
        v_ref[0] = v0.T.astype(BF16)
        v_ref[1] = v1.T.astype(BF16)
    gate_ref[...] = jax.nn.sigmoid(proj(_C_GATE, _C_U))
    u_ref[...] = proj(_C_U, _C_END)


def _inproj(x2, g1, w, qc, ksc, kwc, vc, S):
    T = x2.shape[0]
    TM = TOK_TILE
    nst = S // TM
    tok = lambda w_: pl.BlockSpec((TM, w_), lambda i: (i, 0))
    const = _const_spec
    seq = pl.BlockSpec((TM, LANES), lambda i: (i % nst, 0))
    grp = pl.BlockSpec((N_KV, TM, LANES), lambda i: (0, i, 0))
    grp_t = pl.BlockSpec((N_KV, LANES, TM), lambda i: (0, 0, i))
    return pl.pallas_call(
        _inproj_kernel,
        grid=(T // TM,),
        in_specs=[tok(D_MODEL), const((1, D_MODEL)), const((D_MODEL, _C_END)),
                  const((1, N_HEADS * LANES)), seq, seq, const((1, LANES))],
        out_specs=[tok(N_HEADS * LANES), pl.BlockSpec((2, TM, LANES), lambda i: (0, i, 0)),
                   grp, grp_t, grp, grp_t, tok(LANES), tok(POOL_W)],
        out_shape=[jax.ShapeDtypeStruct((T, N_HEADS * LANES), BF16),
                   jax.ShapeDtypeStruct((2, T, LANES), F32),
                   jax.ShapeDtypeStruct((N_KV, T, LANES), BF16),
                   jax.ShapeDtypeStruct((N_KV, LANES, T), BF16),
                   jax.ShapeDtypeStruct((N_KV, T, LANES), BF16),
                   jax.ShapeDtypeStruct((N_KV, LANES, T), BF16),
                   jax.ShapeDtypeStruct((T, LANES), F32),
                   jax.ShapeDtypeStruct((T, POOL_W), F32)],
        compiler_params=_params(("parallel",)),
    )(x2, g1, w, qc, ksc, kwc, vc)


def _compress_kernel(x_ref, pe_ref, w1_ref, w2_ref, cc_ref, kca_ref, vct_ref):
    nc = x_ref.shape[1] // CMP_STRIDE
    hs = []
    for kv in range(2):
        a = jnp.zeros((nc, LANES), F32)
        b = jnp.zeros((nc, LANES), F32)
        for l in range(CMP_STRIDE):
            xl = x_ref[kv, pl.ds(l, nc, stride=CMP_STRIDE), :]
            a = a + _dot((xl + pe_ref[0, l, kv:kv + 1, :]).astype(BF16), w1_ref[0, l, kv])
            b = b + _dot((xl + pe_ref[1, l, kv:kv + 1, :]).astype(BF16), w1_ref[1, l, kv])
        hs.append(a + pltpu.roll(b, nc - 1, 0))
    h = jnp.concatenate(hs, axis=1)
    h = jax.nn.gelu(h).astype(BF16)
    out = _dot(h, w2_ref[...])
    cc = cc_ref[...]
    for g in range(N_KV):
        kca_ref[0, g] = (out[:, g * LANES:(g + 1) * LANES] + cc).astype(BF16)
        vct_ref[0, g] = out[:, (N_KV + g) * LANES:(N_KV + g + 1) * LANES].T.astype(BF16)


def _compress(kvc3, pe, w1, w2, cc, B):
    S = kvc3.shape[1] // B
    NC = S // CMP_STRIDE
    const = lambda shape: pl.BlockSpec(shape, lambda b: (0,) * len(shape))
    return pl.pallas_call(
        _compress_kernel,
        grid=(B,),
        in_specs=[pl.BlockSpec((2, S, LANES), lambda b: (0, b, 0)), const((2, CMP_STRIDE, 2, LANES)),
                  const((2, CMP_STRIDE, 2, LANES, LANES)),
                  const((2 * N_KV * CMP_HIDDEN, 2 * N_KV * LANES)), const((NC, LANES))],
        out_specs=[pl.BlockSpec((1, N_KV, NC, LANES), lambda b: (b, 0, 0, 0)),
                   pl.BlockSpec((1, N_KV, LANES, NC), lambda b: (b, 0, 0, 0))],
        out_shape=[jax.ShapeDtypeStruct((B, N_KV, NC, LANES), BF16),
                   jax.ShapeDtypeStruct((B, N_KV, LANES, NC), BF16)],
        compiler_params=_params(("parallel",)),
    )(kvc3, pe, w1, w2, cc)


def _nsa_kernel(qa_ref, gate_ref, kca_ref, vct_ref, ksa_ref, vst_ref, kwa_ref, vwt_ref,
                ov_ref, ssl_ref, slv_ref, o_ref, qsel_ref, m_ref, acc_ref, pp_ref, pa_ref, s0a_ref, s0b_ref,
                mx_ref, sh_ref, ex_ref, *, S):
    QB, R, M = Q_BLOCK, GROUP, GROUP * Q_BLOCK
    NC = S // CMP_STRIDE
    n_sel = S // SEL_BLOCK
    top_n = min(SEL_TOP_N, n_sel)
    WK = WINDOW + QB
    q0 = pl.program_id(1) * QB
    s0_refs = (s0a_ref, s0b_ref)

    def heads(a):
        return jnp.concatenate([a] * R, axis=1)

    ce = lax.broadcasted_iota(jnp.int32, (NC, QB), 0) * CMP_STRIDE + (CMP_BLOCK - 1)
    vis = (q0 + lax.broadcasted_iota(jnp.int32, (NC, QB), 1)) >= ce
    vbias = heads(jnp.where(vis, 0.0, NEG))
    any_vis = heads((q0 + lax.broadcasted_iota(jnp.int32, (1, QB), 1)) >= CMP_BLOCK - 1)

    jrow = lax.broadcasted_iota(jnp.int32, (SEL_BLOCK, QB), 0)
    cur = (q0 + lax.broadcasted_iota(jnp.int32, (SEL_BLOCK, QB), 1)) // SEL_BLOCK
    future = jrow > cur
    forced = (jrow == 0) | (jrow == cur) | (jrow == cur - 1)
    sub8 = lax.broadcasted_iota(jnp.int32, (SUBLANES, QB), 0)

    ws = pl.multiple_of(jnp.maximum(q0 - WINDOW, 0), QB)
    dw = (q0 + lax.broadcasted_iota(jnp.int32, (WK, QB), 1)) - (ws + lax.broadcasted_iota(jnp.int32, (WK, QB), 0))
    wbias = heads(jnp.where(dw >= 0, jnp.where(dw < WINDOW, 0.0, NEG), NEG))

    qas = [jnp.concatenate([qa_ref[:, (g * R + r) * LANES:(g * R + r + 1) * LANES] for r in range(R)],
                           axis=0) for g in range(N_KV)]
    s_cs = [_dot_nt(kca_ref[0, g], qas[g]) for g in range(N_KV)]
    s_ws = [_dot_nt(kwa_ref[g, pl.ds(ws, WK), :], qas[g]) for g in range(N_KV)]

    o_cs, imps = [], []
    for g in range(N_KV):
        s = s_cs[g] + vbias
        m = jnp.max(s, axis=0, keepdims=True)
        e = jnp.exp(s - m)
        l = jnp.sum(e, axis=0, keepdims=True)
        p = e * jnp.where(any_vis, 1.0 / l, 0.0)
        o_cs.append(_dot(vct_ref[0, g], p.astype(BF16)))
        psum = p[:, 0:QB] + p[:, QB:2 * QB] + p[:, 2 * QB:3 * QB] + p[:, 3 * QB:4 * QB]
        hi = psum.astype(BF16)
        r1 = psum - hi.astype(F32)
        mid = r1.astype(BF16)
        lo = (r1 - mid.astype(F32)).astype(BF16)
        ov = ov_ref[...]
        imps.append(_dot(ov, hi) + _dot(ov, mid) + _dot(ov, lo))

    def select(g):
        v = jnp.where(future, NEG, jnp.where(forced, FORCE, imps[g]))
        vt = [v[k * SUBLANES:(k + 1) * SUBLANES] for k in range(SEL_BLOCK // SUBLANES)]
        cnt = [jnp.zeros((SUBLANES, QB), jnp.int32) for _ in vt]
        for i in range(n_sel):
            vi = v[i:i + 1, :]
            for k, vk in enumerate(vt):
                if (k + 1) * SUBLANES <= i + 1:
                    beats = jnp.where(vi > vk, 1, 0)
                elif k * SUBLANES > i:
                    beats = jnp.where(vi >= vk, 1, 0)
                else:
                    beats = jnp.where(sub8 > i - k * SUBLANES, jnp.where(vi >= vk, 1, 0),
                                      jnp.where(vi > vk, 1, 0))
                cnt[k] = cnt[k] + beats
        cnt = jnp.concatenate(cnt, axis=0)
        selb = jnp.where(cnt < top_n, jnp.where(v > NEG / 2, 0.0, UNSEL), UNSEL)
        aug = jnp.concatenate([jnp.zeros_like(selb), selb], axis=0).T
        for r in range(R):
            qh = qas[g][r * QB:(r + 1) * QB].astype(F32)
            qsel_ref[g, r * QB:(r + 1) * QB, :] = (
                qh + aug + ssl_ref[g * R + r:g * R + r + 1, :]).astype(BF16)

    def window(g):
        sw = s_ws[g] + wbias
        pw = jnp.exp((sw - jnp.max(sw, axis=0, keepdims=True)).astype(BF16))
        accw = _dot(vwt_ref[g, :, pl.ds(ws, WK)], pw)
        return accw * (1.0 / accw[HEAD_DIM:HEAD_DIM + 1, :])

    def sel_scores(g, k0):
        return _dot_nt(ksa_ref[g, pl.ds(k0, SEL_CHUNK), :], qsel_ref[g])

    select(0)
    o_ws = [window(0)]
    select(1)
    o_ws.append(window(1))

    tvec = heads(q0 + lax.broadcasted_iota(jnp.int32, (1, QB), 1)).astype(F32)

    def bias_cap(g, k_hi):
        return slv_ref[g] * jnp.minimum(tvec, k_hi)

    def run_selected(streaming):
        acc_ref[...] = jnp.zeros(acc_ref.shape, F32)
        pp_ref[...] = jnp.zeros(pp_ref.shape, BF16)
        pa_ref[...] = jnp.ones(pa_ref.shape, F32)
        if streaming:
            kpos0 = lax.broadcasted_iota(jnp.int32, (SEL_BLOCK, QB), 0)
            tq0 = q0 + lax.broadcasted_iota(jnp.int32, (SEL_BLOCK, QB), 1)
            cb0 = heads(jnp.where(kpos0 <= tq0, 0.0, NEG))
            for g in range(N_KV):
                t0 = _dot_nt(ksa_ref[g, 0:SEL_BLOCK, :], qsel_ref[g]) + cb0
                m0 = jnp.max(t0, axis=0, keepdims=True)
                r0 = m0 - bias_cap(g, SEL_BLOCK - 1.0)
                m_ref[g] = r0
                mx_ref[g] = m0
                sh_ref[g] = r0 + bias_cap(g, SEL_CHUNK - 1.0)
            ex_ref[...] = jnp.zeros(ex_ref.shape, F32)
        else:
            m_ref[...] = jnp.full(m_ref.shape, NEG, F32)

        def online(g, sc, k0):
            m_cur = jnp.max(sc, axis=0, keepdims=True)
            m_prev = m_ref[g]
            if not streaming:
                m_new = jnp.maximum(m_prev, m_cur)
                m_ref[g] = m_new
                return jnp.exp((sc - m_new).astype(BF16)), jnp.exp(m_prev - m_new)
            cap = bias_cap(g, (k0 + (SEL_CHUNK - 1)).astype(F32))
            shift = m_prev + cap
            r_cur = m_cur - cap
            m_ref[g] = jnp.maximum(m_prev, r_cur)
            mx_ref[g] = jnp.maximum(mx_ref[g], m_cur)
            ex_ref[g] = jnp.maximum(ex_ref[g], r_cur - m_prev)
            alpha = jnp.exp(sh_ref[g] - shift)
            sh_ref[g] = shift
            return jnp.exp((sc - shift).astype(BF16)), alpha

        def pending_pv(k0):
            acc_ref[1] = pa_ref[...] * acc_ref[1] + _dot(vst_ref[1, :, pl.ds(k0, SEL_CHUNK)], pp_ref[...])

        def step(c, src_ref, dst_ref):
            last = dst_ref is None
            k0 = pl.multiple_of(c * SEL_CHUNK, SEL_CHUNK)
            kprev = pl.multiple_of(jnp.maximum(c - 1, 0) * SEL_CHUNK, SEL_CHUNK)
            s1 = sel_scores(1, k0)
            pending_pv(kprev)
            if not last:
                dst_ref[...] = sel_scores(0, pl.multiple_of(k0 + SEL_CHUNK, SEL_CHUNK))
            s0 = src_ref[...]
            if last:
                kpos = k0 + lax.broadcasted_iota(jnp.int32, (SEL_CHUNK, QB), 0)
                tq = q0 + lax.broadcasted_iota(jnp.int32, (SEL_CHUNK, QB), 1)
                cb = heads(jnp.where(kpos <= tq, 0.0, NEG))
                s0 = s0 + cb
                s1 = s1 + cb
            p0, alpha0 = online(0, s0, k0)
            acc_ref[0] = alpha0 * acc_ref[0] + _dot(vst_ref[0, :, pl.ds(k0, SEL_CHUNK)], p0)
            p1, alpha1 = online(1, s1, k0)
            pp_ref[...] = p1
            pa_ref[...] = alpha1

        def body(c, carry):
            for par in range(2):
                pl.when(c % 2 == par)(functools.partial(step, c, s0_refs[par], s0_refs[1 - par]))
            return carry

        s0_refs[0][...] = sel_scores(0, 0)
        nfull = q0 // SEL_CHUNK
        lax.fori_loop(0, nfull, body, 0)
        for par in range(2):
            pl.when(nfull % 2 == par)(functools.partial(step, nfull, s0_refs[par], None))
        pending_pv(pl.multiple_of(nfull * SEL_CHUNK, SEL_CHUNK))

    run_selected(True)
    unsafe = jnp.maximum(ex_ref[...] - SHIFT_SLACK, sh_ref[...] - mx_ref[...] - RANGE_SLACK)
    pl.when(jnp.max(unsafe) > 0.0)(functools.partial(run_selected, False))


    gt = gate_ref[...].T
    outs = []
    for g in range(N_KV):
        acc = acc_ref[g]
        o_s = acc * (1.0 / acc[HEAD_DIM:HEAD_DIM + 1, :])
        o_c, o_w = o_cs[g], o_ws[g]
        for r in range(R):
            col = (g * R + r) * 3
            cols = slice(r * QB, (r + 1) * QB)
            o = (gt[col:col + 1] * o_c[:HEAD_DIM, cols] + gt[col + 1:col + 2] * o_s[:HEAD_DIM, cols]
                 + gt[col + 2:col + 3] * o_w[:HEAD_DIM, cols])
            outs.append(o)
    o_ref[...] = jnp.concatenate(outs, axis=0).T.astype(BF16)


def _nsa(qa, gates, kca, vct, ksa, vst, kwa, vwt, ov, ssl, slv, B, S):
    NC = S // CMP_STRIDE
    nqb = S // Q_BLOCK
    M = GROUP * Q_BLOCK
    tok = lambda w_: pl.BlockSpec((Q_BLOCK, w_), lambda b, i: (b * nqb + i, 0))
    const = lambda shape: pl.BlockSpec(shape, lambda b, i: (0,) * len(shape))
    seq = pl.BlockSpec((N_KV, S, LANES), lambda b, i: (0, b, 0))
    seq_t = pl.BlockSpec((N_KV, LANES, S), lambda b, i: (0, 0, b))
    return pl.pallas_call(
        functools.partial(_nsa_kernel, S=S),
        grid=(B, nqb),
        in_specs=[tok(N_HEADS * LANES), tok(LANES),
                  pl.BlockSpec((1, N_KV, NC, LANES), lambda b, i: (b, 0, 0, 0)),
                  pl.BlockSpec((1, N_KV, LANES, NC), lambda b, i: (b, 0, 0, 0)),
                  seq, seq_t, seq, seq_t,
                  const((SEL_BLOCK, NC)), const((N_HEADS, LANES)), const((N_KV, 1, M))],
        out_specs=tok(ATTN_W),
        out_shape=jax.ShapeDtypeStruct((B * S, ATTN_W), BF16),
        scratch_shapes=[pltpu.VMEM((N_KV, M, LANES), BF16), pltpu.VMEM((N_KV, 1, M), F32),
                        pltpu.VMEM((N_KV, LANES, M), F32),
                        pltpu.VMEM((SEL_CHUNK, M), BF16), pltpu.VMEM((1, M), F32),
                        pltpu.VMEM((SEL_CHUNK, M), F32), pltpu.VMEM((SEL_CHUNK, M), F32)]
                       + [pltpu.VMEM((N_KV, 1, M), F32)] * 3,
        compiler_params=_params(("parallel", "arbitrary")),
    )(qa, gates, kca, vct, ksa, vst, kwa, vwt, ov, ssl, slv)


def _mix_kernel(x_ref, attn_ref, u_ref, uh_ref, g1_ref, wm_ref, wap_ref, wpool_ref, ps_ref,
                wpp_ref, wout_ref, o_ref, *, nst):
    TM = x_ref.shape[0]
    seq_tile = pl.program_id(0) % nst
    x = x_ref[...]
    h = _rmsnorm(x, g1_ref[...]).astype(BF16)
    merge = jax.nn.sigmoid(_dot(h, wm_ref[...]))
    a = _dot(attn_ref[...], wap_ref[...])

    halo = jnp.where(seq_tile == 0, 0.0, uh_ref[...])
    ue = jnp.concatenate([halo, u_ref[...]], axis=0)
    pos = seq_tile * TM + lax.broadcasted_iota(jnp.int32, (TM, 1), 0)
    ys = []
    for gi, w in enumerate(POOL_WINDOWS):
        ug = ue[:, gi * POOL_DIM:(gi + 1) * POOL_DIM]
        s = ug
        sh = 1
        while sh < w:
            s = s + pltpu.roll(s, sh, 0)
            sh *= 2
        cnt = jnp.minimum(pos + 1, w).astype(F32)
        p = s[POOL_HALO:] / cnt - ug[POOL_HALO:]
        ys.append(_dot(p.astype(BF16), wpool_ref[gi]))
    y = jnp.concatenate(ys, axis=1) * ps_ref[...]
    b = _dot(y.astype(BF16), wpp_ref[...])
    mix = merge[:, :D_MODEL] * a + merge[:, D_MODEL:] * b
    o_ref[...] = x + _dot(mix.astype(BF16), wout_ref[...])


def _mix(x2, attn, u, g1, wm, wap, wpool, ps, wpp, wout, S):
    T = x2.shape[0]
    TM = TOK_TILE
    nst = S // TM
    tok = lambda w_: pl.BlockSpec((TM, w_), lambda i: (i, 0))
    const = _const_spec
    halo = pl.BlockSpec((POOL_HALO, POOL_W), lambda i: (jnp.maximum(i * (TM // POOL_HALO) - 1, 0), 0))
    return pl.pallas_call(
        functools.partial(_mix_kernel, nst=nst),
        grid=(T // TM,),
        in_specs=[tok(D_MODEL), tok(ATTN_W), tok(POOL_W), halo, const((1, D_MODEL)),
                  const((D_MODEL, 2 * D_MODEL)), const((ATTN_W, D_MODEL)),
                  const((POOL_GROUPS, POOL_DIM, POOL_DIM)), const((1, POOL_W)),
                  const((POOL_W, D_MODEL)), const((D_MODEL, D_MODEL))],
        out_specs=tok(D_MODEL),
        out_shape=jax.ShapeDtypeStruct((T, D_MODEL), F32),
        compiler_params=_params(("parallel",)),
    )(x2, attn, u, u, g1, wm, wap, wpool, ps, wpp, wout)


def _ffn_kernel(x_ref, g2_ref, w1_ref, w2_ref, gf_ref, o_ref, *, final):
    x = x_ref[...]
    h = _rmsnorm(x, g2_ref[...]).astype(BF16)
    acc = x
    for c in range(D_FF // FF_CHUNK):
        cols = slice(c * FF_CHUNK, (c + 1) * FF_CHUNK)
        t = jnp.square(jnp.maximum(_dot(h, w1_ref[:, cols]), 0.0)).astype(BF16)
        acc = acc + _dot(t, w2_ref[cols, :])
    if final:
        acc = _rmsnorm(acc, gf_ref[...])
    o_ref[...] = acc


def _ffn(x2, g2, w1, w2, gf, final):
    T = x2.shape[0]
    TM = TOK_TILE
    tok = pl.BlockSpec((TM, D_MODEL), lambda i: (i, 0))
    const = _const_spec
    return pl.pallas_call(
        functools.partial(_ffn_kernel, final=final),
        grid=(T // TM,),
        in_specs=[tok, const((1, D_MODEL)), const((D_MODEL, D_FF)), const((D_FF, D_MODEL)),
                  const((1, D_MODEL))],
        out_specs=tok,
        out_shape=jax.ShapeDtypeStruct((T, D_MODEL), F32),
        compiler_params=_params(("parallel",)),
    )(x2, g2, w1, w2, gf)


def _pack_w_in(w_in_l):
    scale = HEAD_DIM ** -0.5
    wq = w_in_l[:, :ATTN_W] * scale
    o_kv = ATTN_W
    o_gate = o_kv + 3 * KV_W
    o_pool = o_gate + 3 * N_HEADS
    o_merge = o_pool + POOL_W
    wgate = jnp.pad(w_in_l[:, o_gate:o_pool], ((0, 0), (0, LANES - 3 * N_HEADS)))
    w = jnp.concatenate([wq, w_in_l[:, o_kv:o_gate], wgate, w_in_l[:, o_pool:o_merge]], axis=1)
    return w.astype(BF16), w_in_l[:, o_merge:].astype(BF16)


def _pack_compress(pe_l, w1_l, w2_l):
    eye = jnp.eye(N_KV, dtype=F32)
    half = CMP_BLOCK // 2
    w1r = w1_l.reshape(2, 2, half, HEAD_DIM, CMP_HIDDEN)
    w1big = jnp.einsum('khldm,gG->hlkgdGm', w1r, eye)
    w1big = w1big.reshape(2, half, 2, N_KV * HEAD_DIM, N_KV * CMP_HIDDEN)
    per = pe_l.reshape(2, 2, half, HEAD_DIM)
    pebig = jnp.broadcast_to(per.transpose(1, 2, 0, 3)[:, :, :, None, :],
                             (2, half, 2, N_KV, HEAD_DIM)).reshape(2, half, 2, N_KV * HEAD_DIM)
    w2p = jnp.pad(w2_l, ((0, 0), (0, 0), (0, LANES - HEAD_DIM)))
    w2big = jnp.einsum('kme,kK,gG->kgmKGe', w2p, eye, eye)
    w2big = w2big.reshape(2 * N_KV * CMP_HIDDEN, 2 * N_KV * LANES)
    return pebig, w1big.astype(BF16), w2big.astype(BF16)


def _constants(S):
    NC = S // CMP_STRIDE
    n_sel = S // SEL_BLOCK
    pos = np.arange(S)
    ksc = np.zeros((S, LANES), np.float32)
    ksc[:, HEAD_DIM] = pos % SEL_BLOCK
    blk = pos // SEL_BLOCK
    sel_rows = blk >= 1
    ksc[pos[sel_rows], HEAD_DIM + blk[sel_rows]] = 1.0
    kwc = np.zeros((S, LANES), np.float32)
    kwc[:, HEAD_DIM] = pos % SEL_BLOCK
    kwc[:, HEAD_DIM + 1] = blk
    vc = np.zeros((1, LANES), np.float32)
    vc[0, HEAD_DIM] = 1.0
    qc = np.zeros((1, N_HEADS * LANES), np.float32)
    ssl = np.zeros((N_HEADS, LANES), np.float32)
    for h in range(N_HEADS):
        qc[0, h * LANES + HEAD_DIM] = SLOPES[h]
        qc[0, h * LANES + HEAD_DIM + 1] = SLOPES[h] * SEL_BLOCK
        ssl[h, HEAD_DIM + 2:] = SLOPES[h] * SEL_BLOCK * np.arange(2, SEL_BLOCK)
    c = np.arange(NC)
    cc = np.zeros((NC, LANES), np.float32)
    cstart = c * CMP_STRIDE
    cc[:, HEAD_DIM] = cstart % SEL_BLOCK
    cc[:, HEAD_DIM + 1] = cstart // SEL_BLOCK
    slv = np.repeat(np.asarray(SLOPES, np.float32).reshape(N_KV, 1, GROUP), Q_BLOCK, axis=2)
    j = np.arange(SEL_BLOCK)
    ovt = ((cstart[None, :] <= j[:, None] * SEL_BLOCK + SEL_BLOCK - 1)
           & (cstart[None, :] + CMP_BLOCK - 1 >= j[:, None] * SEL_BLOCK)
           & (j[:, None] < n_sel) & (c[None, :] < NC - 1)).astype(np.float32)
    return (jnp.asarray(qc), jnp.asarray(ksc), jnp.asarray(kwc), jnp.asarray(vc), jnp.asarray(cc),
            jnp.asarray(ovt, BF16), jnp.asarray(ssl), jnp.asarray(slv))


def kernel(x, norm1_g, w_in, cmp_pe, cmp_w1, cmp_w2, w_attn_proj, w_pool, pool_scale,
           w_pool_proj, w_out, norm2_g, w_ff1, w_ff2, final_g):
    B, S, _ = x.shape
    depth = w_in.shape[0]
    assert S % SEL_CHUNK == 0 and S % TOK_TILE == 0 and S // SEL_BLOCK <= SEL_BLOCK and S >= WINDOW + Q_BLOCK
    qc, ksc, kwc, vc, cc, ovt, ssl, slv = _constants(S)
    x2 = x.reshape(B * S, D_MODEL)
    for l in range(depth):
        w_pack, w_merge = _pack_w_in(w_in[l])
        pebig, w1big, w2big = _pack_compress(cmp_pe[l], cmp_w1[l], cmp_w2[l])
        g1 = norm1_g[l].reshape(1, D_MODEL)
        qa, kvc, ksa, vst, kwa, vwt, gates, u = _inproj(x2, g1, w_pack, qc, ksc, kwc, vc, S)
        kca, vct = _compress(kvc, pebig, w1big, w2big, cc, B)
        attn = _nsa(qa, gates, kca, vct, ksa, vst, kwa, vwt, ovt, ssl, slv, B, S)
        x2 = _mix(x2, attn, u, g1, w_merge, w_attn_proj[l].astype(BF16), w_pool[l].astype(BF16),
                  pool_scale[l].reshape(1, POOL_W), w_pool_proj[l].astype(BF16),
                  w_out[l].astype(BF16), S)
        x2 = _ffn(x2, norm2_g[l].reshape(1, D_MODEL), w_ff1[l].astype(BF16), w_ff2[l].astype(BF16),
                  final_g.reshape(1, D_MODEL), final=(l == depth - 1))
    return x2.reshape(B, S, D_MODEL)
```

```python
import functools

import numpy as np
import jax
import jax.numpy as jnp
from jax import lax
from jax.experimental import pallas as pl
from jax.experimental.pallas import tpu as pltpu

F32 = jnp.float32
BF16 = jnp.bfloat16

D_MODEL = 1024
N_HEADS = 8
HEAD_DIM = 64
N_KV = 2
GROUP = N_HEADS // N_KV
ATTN_W = N_HEADS * HEAD_DIM
KV_W = 2 * N_KV * HEAD_DIM
CMP_BLOCK = 32
CMP_STRIDE = 16
CMP_HIDDEN = 64
SEL_BLOCK = 64
SEL_TOP_N = 16
WINDOW = 512
POOL_GROUPS = 4
POOL_DIM = 128
POOL_W = POOL_GROUPS * POOL_DIM
POOL_WINDOWS = (2, 4, 8, 16)
POOL_HALO = 16
D_FF = 4 * D_MODEL
EPS = 1e-6
NEG = -1e30
FORCE = 1e6
UNSEL = -1e9
SLOPES = tuple(2.0 ** (-8.0 * (h + 1) / N_HEADS) for h in range(N_HEADS))

LANES = 128
SUBLANES = 8
Q_BLOCK = 256
SEL_CHUNK = 512
TOK_TILE = 1024
FF_CHUNK = 1024
VMEM_LIMIT = 56 * 1024 * 1024
SHIFT_SLACK = 12.0
RANGE_SLACK = 12.0

_C_Q = 0
_C_KC = _C_Q + ATTN_W
_C_KS = _C_KC + KV_W
_C_VS = _C_KS + LANES
_C_KW = _C_VS + LANES
_C_VW = _C_KW + LANES
_C_GATE = _C_VW + LANES
_C_U = _C_GATE + LANES
_C_END = _C_U + POOL_W


def _dot(a, b):
    return jnp.dot(a, b, preferred_element_type=F32)


def _dot_nt(a, b):
    return lax.dot_general(a, b, (((1,), (1,)), ((), ())), preferred_element_type=F32)


def _rmsnorm(x, g):
    return x * lax.rsqrt(jnp.mean(x * x, axis=-1, keepdims=True) + EPS) * g


def _const_spec(shape):
    return pl.BlockSpec(shape, lambda i: (0,) * len(shape), pipeline_mode=pl.Buffered(1))


def _params(sem):
    return pltpu.CompilerParams(dimension_semantics=sem, vmem_limit_bytes=VMEM_LIMIT)


def _inproj_kernel(x_ref, g_ref, w_ref, qc_ref, ksc_ref, kwc_ref, vc_ref,
                   qa_ref, kvc_ref, ksa_ref, vst_ref, kwa_ref, vwt_ref, gate_ref, u_ref):
    x = x_ref[...]
    h = _rmsnorm(x, g_ref[...]).astype(BF16)

    r_all = _dot(h, w_ref[...])

    def proj(a, b):
        return r_all[:, a:b]

    kvc_ref[0] = proj(_C_KC, _C_KC + LANES)
    kvc_ref[1] = proj(_C_KC + LANES, _C_KS)
    low = lax.broadcasted_iota(jnp.int32, (x.shape[0], LANES), 1) < HEAD_DIM

    def split(r, c0, c1):
        return jnp.where(low, r, c0), jnp.where(low, pltpu.roll(r, HEAD_DIM, 1), c1)

    rq = proj(_C_Q, _C_KC)
    for j in range(N_HEADS // 2):
        lanes = [slice((2 * j + i) * LANES, (2 * j + i + 1) * LANES) for i in range(2)]
        q0, q1 = split(rq[:, j * LANES:(j + 1) * LANES], qc_ref[:, lanes[0]], qc_ref[:, lanes[1]])
        qa_ref[:, lanes[0]] = q0.astype(BF16)
        qa_ref[:, lanes[1]] = q1.astype(BF16)
    for k_ref, v_ref, kc_ref, c0 in ((ksa_ref, vst_ref, ksc_ref, _C_KS), (kwa_ref, vwt_ref, kwc_ref, _C_KW)):
        k0, k1 = split(proj(c0, c0 + LANES), kc_ref[...], kc_ref[...])
        k_ref[0] = k0.astype(BF16)
        k_ref[1] = k1.astype(BF16)
        v0, v1 = split(proj(c0 + LANES, c0 + 2 * LANES), vc_ref[...], vc_ref[...])
        v_ref[0] = v0.T.astype(BF16)
        v_ref[1] = v1.T.astype(BF16)
    gate_ref[...] = jax.nn.sigmoid(proj(_C_GATE, _C_U))
    u_ref[...] = proj(_C_U, _C_END)


def _inproj(x2, g1, w, qc, ksc, kwc, vc, S):
    T = x2.shape[0]
    TM = TOK_TILE
    nst = S // TM
    tok = lambda w_: pl.BlockSpec((TM, w_), lambda i: (i, 0))
    const = _const_spec
    seq = pl.BlockSpec((TM, LANES), lambda i: (i % nst, 0))
    grp = pl.BlockSpec((N_KV, TM, LANES), lambda i: (0, i, 0))
    grp_t = pl.BlockSpec((N_KV, LANES, TM), lambda i: (0, 0, i))
    return pl.pallas_call(
        _inproj_kernel,
        grid=(T // TM,),
        in_specs=[tok(D_MODEL), const((1, D_MODEL)), const((D_MODEL, _C_END)),
                  const((1, N_HEADS * LANES)), seq, seq, const((1, LANES))],
        out_specs=[tok(N_HEADS * LANES), pl.BlockSpec((2, TM, LANES), lambda i: (0, i, 0)),
                   grp, grp_t, grp, grp_t, tok(LANES), tok(POOL_W)],
        out_shape=[jax.ShapeDtypeStruct((T, N_HEADS * LANES), BF16),
                   jax.ShapeDtypeStruct((2, T, LANES), F32),
                   jax.ShapeDtypeStruct((N_KV, T, LANES), BF16),
                   jax.ShapeDtypeStruct((N_KV, LANES, T), BF16),
                   jax.ShapeDtypeStruct((N_KV, T, LANES), BF16),
                   jax.ShapeDtypeStruct((N_KV, LANES, T), BF16),
                   jax.ShapeDtypeStruct((T, LANES), F32),
                   jax.ShapeDtypeStruct((T, POOL_W), F32)],
        compiler_params=_params(("parallel",)),
    )(x2, g1, w, qc, ksc, kwc, vc)


def _compress_kernel(x_ref, pe_ref, w1_ref, w2_ref, cc_ref, kca_ref, vct_ref):
    nc = x_ref.shape[1] // CMP_STRIDE
    hs = []
    for kv in range(2):
        a = jnp.zeros((nc, LANES), F32)
        b = jnp.zeros((nc, LANES), F32)
        for l in range(CMP_STRIDE):
            xl = x_ref[kv, pl.ds(l, nc, stride=CMP_STRIDE), :]
            a = a + _dot((xl + pe_ref[0, l, kv:kv + 1, :]).astype(BF16), w1_ref[0, l, kv])
            b = b + _dot((xl + pe_ref[1, l, kv:kv + 1, :]).astype(BF16), w1_ref[1, l, kv])
        hs.append(a + pltpu.roll(b, nc - 1, 0))
    h = jnp.concatenate(hs, axis=1)
    h = jax.nn.gelu(h).astype(BF16)
    out = _dot(h, w2_ref[...])
    cc = cc_ref[...]
    for g in range(N_KV):
        kca_ref[0, g] = (out[:, g * LANES:(g + 1) * LANES] + cc).astype(BF16)
        vct_ref[0, g] = out[:, (N_KV + g) * LANES:(N_KV + g + 1) * LANES].T.astype(BF16)


def _compress(kvc3, pe, w1, w2, cc, B):
    S = kvc3.shape[1] // B
    NC = S // CMP_STRIDE
    const = lambda shape: pl.BlockSpec(shape, lambda b: (0,) * len(shape))
    return pl.pallas_call(
        _compress_kernel,
        grid=(B,),
        in_specs=[pl.BlockSpec((2, S, LANES), lambda b: (0, b, 0)), const((2, CMP_STRIDE, 2, LANES)),
                  const((2, CMP_STRIDE, 2, LANES, LANES)),
                  const((2 * N_KV * CMP_HIDDEN, 2 * N_KV * LANES)), const((NC, LANES))],
        out_specs=[pl.BlockSpec((1, N_KV, NC, LANES), lambda b: (b, 0, 0, 0)),
                   pl.BlockSpec((1, N_KV, LANES, NC), lambda b: (b, 0, 0, 0))],
        out_shape=[jax.ShapeDtypeStruct((B, N_KV, NC, LANES), BF16),
                   jax.ShapeDtypeStruct((B, N_KV, LANES, NC), BF16)],
        compiler_params=_params(("parallel",)),
    )(kvc3, pe, w1, w2, cc)


def _nsa_kernel(qa_ref, gate_ref, kca_ref, vct_ref, ksa_ref, vst_ref, kwa_ref, vwt_ref,
                ov_ref, ssl_ref, slv_ref, o_ref, qsel_ref, m_ref, acc_ref, pp_ref, pa_ref, s0a_ref, s0b_ref,
                mx_ref, sh_ref, ex_ref, wx_ref, ow_ref, *, S):
    QB, R, M = Q_BLOCK, GROUP, GROUP * Q_BLOCK
    NC = S // CMP_STRIDE
    n_sel = S // SEL_BLOCK
    top_n = min(SEL_TOP_N, n_sel)
    WK = WINDOW + QB
    q0 = pl.program_id(1) * QB
    s0_refs = (s0a_ref, s0b_ref)

    def heads(a):
        return jnp.concatenate([a] * R, axis=1)

    ce = lax.broadcasted_iota(jnp.int32, (NC, QB), 0) * CMP_STRIDE + (CMP_BLOCK - 1)
    vis = (q0 + lax.broadcasted_iota(jnp.int32, (NC, QB), 1)) >= ce
    vbias = heads(jnp.where(vis, 0.0, NEG))
    any_vis = heads((q0 + lax.broadcasted_iota(jnp.int32, (1, QB), 1)) >= CMP_BLOCK - 1)

    jrow = lax.broadcasted_iota(jnp.int32, (SEL_BLOCK, QB), 0)
    cur = (q0 + lax.broadcasted_iota(jnp.int32, (SEL_BLOCK, QB), 1)) // SEL_BLOCK
    future = jrow > cur
    forced = (jrow == 0) | (jrow == cur) | (jrow == cur - 1)
    sub8 = lax.broadcasted_iota(jnp.int32, (SUBLANES, QB), 0)

    ws = pl.multiple_of(jnp.maximum(q0 - WINDOW, 0), QB)

    qas = [jnp.concatenate([qa_ref[:, (g * R + r) * LANES:(g * R + r + 1) * LANES] for r in range(R)],
                           axis=0) for g in range(N_KV)]
    s_cs = [_dot_nt(kca_ref[0, g], qas[g]) for g in range(N_KV)]

    o_cs, imps = [], []
    for g in range(N_KV):
        s = s_cs[g] + vbias
        m = jnp.max(s, axis=0, keepdims=True)
        e = jnp.exp(s - m)
        l = jnp.sum(e, axis=0, keepdims=True)
        p = e * jnp.where(any_vis, 1.0 / l, 0.0)
        o_cs.append(_dot(vct_ref[0, g], p.astype(BF16)))
        psum = p[:, 0:QB] + p[:, QB:2 * QB] + p[:, 2 * QB:3 * QB] + p[:, 3 * QB:4 * QB]
        hi = psum.astype(BF16)
        r1 = psum - hi.astype(F32)
        mid = r1.astype(BF16)
        lo = (r1 - mid.astype(F32)).astype(BF16)
        ov = ov_ref[...]
        imps.append(_dot(ov, hi) + _dot(ov, mid) + _dot(ov, lo))

    def select(g):
        v = jnp.where(future, NEG, jnp.where(forced, FORCE, imps[g]))
        vt = [v[k * SUBLANES:(k + 1) * SUBLANES] for k in range(SEL_BLOCK // SUBLANES)]
        cnt = [jnp.zeros((SUBLANES, QB), jnp.int32) for _ in vt]
        for i in range(n_sel):
            vi = v[i:i + 1, :]
            for k, vk in enumerate(vt):
                if (k + 1) * SUBLANES <= i + 1:
                    beats = jnp.where(vi > vk, 1, 0)
                elif k * SUBLANES > i:
                    beats = jnp.where(vi >= vk, 1, 0)
                else:
                    beats = jnp.where(sub8 > i - k * SUBLANES, jnp.where(vi >= vk, 1, 0),
                                      jnp.where(vi > vk, 1, 0))
                cnt[k] = cnt[k] + beats
        cnt = jnp.concatenate(cnt, axis=0)
        selb = jnp.where(cnt < top_n, jnp.where(v > NEG / 2, 0.0, UNSEL), UNSEL)
        aug = jnp.concatenate([jnp.zeros_like(selb), selb], axis=0).T
        for r in range(R):
            qh = qas[g][r * QB:(r + 1) * QB].astype(F32)
            qsel_ref[g, r * QB:(r + 1) * QB, :] = (
                qh + aug + ssl_ref[g * R + r:g * R + r + 1, :]).astype(BF16)

    def window(g, streaming):
        def positions(n, k0):
            return (q0 + lax.broadcasted_iota(jnp.int32, (n, QB), 1),
                    k0 + lax.broadcasted_iota(jnp.int32, (n, QB), 0))

        if not streaming:
            tq, kpos = positions(WK, ws)
            dw = tq - kpos
            sw = (_dot_nt(kwa_ref[g, pl.ds(ws, WK), :], qas[g])
                  + heads(jnp.where(dw >= 0, jnp.where(dw < WINDOW, 0.0, NEG), NEG)))
            pw = jnp.exp((sw - jnp.max(sw, axis=0, keepdims=True)).astype(BF16))
            accw = _dot(vwt_ref[g, :, pl.ds(ws, WK)], pw)
        else:
            qs = pl.multiple_of(q0, QB)
            tq, kd = positions(QB, qs)
            sd = (_dot_nt(kwa_ref[g, pl.ds(qs, QB), :], qas[g])
                  + heads(jnp.where(kd <= tq, 0.0, NEG)))
            md = jnp.max(sd, axis=0, keepdims=True)
            pd = jnp.exp((sd - md).astype(BF16))
            tq, kpos = positions(WINDOW, ws)
            far = jnp.where(kpos < q0, jnp.where(tq - kpos < WINDOW, 0.0, NEG), NEG)
            sf = _dot_nt(kwa_ref[g, pl.ds(ws, WINDOW), :], qas[g]) + heads(far)
            wx_ref[g] = jnp.max(sf, axis=0, keepdims=True) - md
            pf = jnp.exp((sf - md).astype(BF16))
            accw = (_dot(vwt_ref[g, :, pl.ds(qs, QB)], pd) + _dot(vwt_ref[g, :, pl.ds(ws, WINDOW)], pf))
        ow_ref[g] = accw * (1.0 / accw[HEAD_DIM:HEAD_DIM + 1, :])

    def sel_scores(g, k0):
        return _dot_nt(ksa_ref[g, pl.ds(k0, SEL_CHUNK), :], qsel_ref[g])

    select(0)
    window(0, True)
    select(1)
    window(1, True)

    tvec = heads(q0 + lax.broadcasted_iota(jnp.int32, (1, QB), 1)).astype(F32)

    def bias_cap(g, k_hi):
        return slv_ref[g] * jnp.minimum(tvec, k_hi)

    def run_selected(streaming):
        acc_ref[...] = jnp.zeros(acc_ref.shape, F32)
        pp_ref[...] = jnp.zeros(pp_ref.shape, BF16)
        pa_ref[...] = jnp.ones(pa_ref.shape, F32)
        if streaming:
            kpos0 = lax.broadcasted_iota(jnp.int32, (SEL_BLOCK, QB), 0)
            tq0 = q0 + lax.broadcasted_iota(jnp.int32, (SEL_BLOCK, QB), 1)
            cb0 = heads(jnp.where(kpos0 <= tq0, 0.0, NEG))
            for g in range(N_KV):
                t0 = _dot_nt(ksa_ref[g, 0:SEL_BLOCK, :], qsel_ref[g]) + cb0
                m0 = jnp.max(t0, axis=0, keepdims=True)
                r0 = m0 - bias_cap(g, SEL_BLOCK - 1.0)
                m_ref[g] = r0
                mx_ref[g] = m0
                sh_ref[g] = r0 + bias_cap(g, SEL_CHUNK - 1.0)
            ex_ref[...] = jnp.zeros(ex_ref.shape, F32)
        else:
            m_ref[...] = jnp.full(m_ref.shape, NEG, F32)

        def online(g, sc, k0):
            m_cur = jnp.max(sc, axis=0, keepdims=True)
            m_prev = m_ref[g]
            if not streaming:
                m_new = jnp.maximum(m_prev, m_cur)
                m_ref[g] = m_new
                return jnp.exp((sc - m_new).astype(BF16)), jnp.exp(m_prev - m_new)
            cap = bias_cap(g, (k0 + (SEL_CHUNK - 1)).astype(F32))
            shift = m_prev + cap
            r_cur = m_cur - cap
            m_ref[g] = jnp.maximum(m_prev, r_cur)
            mx_ref[g] = jnp.maximum(mx_ref[g], m_cur)
            ex_ref[g] = jnp.maximum(ex_ref[g], r_cur - m_prev)
            alpha = jnp.exp(sh_ref[g] - shift)
            sh_ref[g] = shift
            return jnp.exp((sc - shift).astype(BF16)), alpha

        def pending_pv(k0):
            acc_ref[1] = pa_ref[...] * acc_ref[1] + _dot(vst_ref[1, :, pl.ds(k0, SEL_CHUNK)], pp_ref[...])

        def step(c, src_ref, dst_ref):
            last = dst_ref is None
            k0 = pl.multiple_of(c * SEL_CHUNK, SEL_CHUNK)
            kprev = pl.multiple_of(jnp.maximum(c - 1, 0) * SEL_CHUNK, SEL_CHUNK)
            s1 = sel_scores(1, k0)
            pending_pv(kprev)
            if not last:
                dst_ref[...] = sel_scores(0, pl.multiple_of(k0 + SEL_CHUNK, SEL_CHUNK))
            s0 = src_ref[...]
            if last:
                kpos = k0 + lax.broadcasted_iota(jnp.int32, (SEL_CHUNK, QB), 0)
                tq = q0 + lax.broadcasted_iota(jnp.int32, (SEL_CHUNK, QB), 1)
                cb = heads(jnp.where(kpos <= tq, 0.0, NEG))
                s0 = s0 + cb
                s1 = s1 + cb
            p0, alpha0 = online(0, s0, k0)
            acc_ref[0] = alpha0 * acc_ref[0] + _dot(vst_ref[0, :, pl.ds(k0, SEL_CHUNK)], p0)
            p1, alpha1 = online(1, s1, k0)
            pp_ref[...] = p1
            pa_ref[...] = alpha1

        def body(c, carry):
            for par in range(2):
                pl.when(c % 2 == par)(functools.partial(step, c, s0_refs[par], s0_refs[1 - par]))
            return carry

        s0_refs[0][...] = sel_scores(0, 0)
        nfull = q0 // SEL_CHUNK
        lax.fori_loop(0, nfull, body, 0)
        for par in range(2):
            pl.when(nfull % 2 == par)(functools.partial(step, nfull, s0_refs[par], None))
        pending_pv(pl.multiple_of(nfull * SEL_CHUNK, SEL_CHUNK))

    run_selected(True)
    unsafe = jnp.maximum(jnp.maximum(ex_ref[...], wx_ref[...]) - SHIFT_SLACK,
                         sh_ref[...] - mx_ref[...] - RANGE_SLACK)

    @pl.when(jnp.max(unsafe) > 0.0)
    def _():
        for g in range(N_KV):
            window(g, False)
        run_selected(False)


    gt = gate_ref[...].T
    outs = []
    for g in range(N_KV):
        acc = acc_ref[g]
        o_s = acc * (1.0 / acc[HEAD_DIM:HEAD_DIM + 1, :])
        o_c, o_w = o_cs[g], ow_ref[g]
        for r in range(R):
            col = (g * R + r) * 3
            cols = slice(r * QB, (r + 1) * QB)
            o = (gt[col:col + 1] * o_c[:HEAD_DIM, cols] + gt[col + 1:col + 2] * o_s[:HEAD_DIM, cols]
                 + gt[col + 2:col + 3] * o_w[:HEAD_DIM, cols])
            outs.append(o)
    o_ref[...] = jnp.concatenate(outs, axis=0).T.astype(BF16)


def _nsa(qa, gates, kca, vct, ksa, vst, kwa, vwt, ov, ssl, slv, B, S):
    NC = S // CMP_STRIDE
    nqb = S // Q_BLOCK
    M = GROUP * Q_BLOCK
    tok = lambda w_: pl.BlockSpec((Q_BLOCK, w_), lambda b, i: (b * nqb + i, 0))
    const = lambda shape: pl.BlockSpec(shape, lambda b, i: (0,) * len(shape))
    seq = pl.BlockSpec((N_KV, S, LANES), lambda b, i: (0, b, 0))
    seq_t = pl.BlockSpec((N_KV, LANES, S), lambda b, i: (0, 0, b))
    return pl.pallas_call(
        functools.partial(_nsa_kernel, S=S),
        grid=(B, nqb),
        in_specs=[tok(N_HEADS * LANES), tok(LANES),
                  pl.BlockSpec((1, N_KV, NC, LANES), lambda b, i: (b, 0, 0, 0)),
                  pl.BlockSpec((1, N_KV, LANES, NC), lambda b, i: (b, 0, 0, 0)),
                  seq, seq_t, seq, seq_t,
                  const((SEL_BLOCK, NC)), const((N_HEADS, LANES)), const((N_KV, 1, M))],
        out_specs=tok(ATTN_W),
        out_shape=jax.ShapeDtypeStruct((B * S, ATTN_W), BF16),
        scratch_shapes=[pltpu.VMEM((N_KV, M, LANES), BF16), pltpu.VMEM((N_KV, 1, M), F32),
                        pltpu.VMEM((N_KV, LANES, M), F32),
                        pltpu.VMEM((SEL_CHUNK, M), BF16), pltpu.VMEM((1, M), F32),
                        pltpu.VMEM((SEL_CHUNK, M), F32), pltpu.VMEM((SEL_CHUNK, M), F32)]
                       + [pltpu.VMEM((N_KV, 1, M), F32)] * 4 + [pltpu.VMEM((N_KV, LANES, M), F32)],
        compiler_params=_params(("parallel", "arbitrary")),
    )(qa, gates, kca, vct, ksa, vst, kwa, vwt, ov, ssl, slv)


def _mix_kernel(x_ref, attn_ref, u_ref, uh_ref, g1_ref, wm_ref, wap_ref, wpool_ref, ps_ref,
                wpp_ref, wout_ref, o_ref, *, nst):
    TM = x_ref.shape[0]
    seq_tile = pl.program_id(0) % nst
    x = x_ref[...]
    h = _rmsnorm(x, g1_ref[...]).astype(BF16)
    merge = jax.nn.sigmoid(_dot(h, wm_ref[...]))
    a = _dot(attn_ref[...], wap_ref[...])

    halo = jnp.where(seq_tile == 0, 0.0, uh_ref[...])
    ue = jnp.concatenate([halo, u_ref[...]], axis=0)
    pos = seq_tile * TM + lax.broadcasted_iota(jnp.int32, (TM, 1), 0)
    ys = []
    for gi, w in enumerate(POOL_WINDOWS):
        ug = ue[:, gi * POOL_DIM:(gi + 1) * POOL_DIM]
        s = ug
        sh = 1
        while sh < w:
            s = s + pltpu.roll(s, sh, 0)
            sh *= 2
        cnt = jnp.minimum(pos + 1, w).astype(F32)
        p = s[POOL_HALO:] / cnt - ug[POOL_HALO:]
        ys.append(_dot(p.astype(BF16), wpool_ref[gi]))
    y = jnp.concatenate(ys, axis=1) * ps_ref[...]
    b = _dot(y.astype(BF16), wpp_ref[...])
    mix = merge[:, :D_MODEL] * a + merge[:, D_MODEL:] * b
    o_ref[...] = x + _dot(mix.astype(BF16), wout_ref[...])


def _mix(x2, attn, u, g1, wm, wap, wpool, ps, wpp, wout, S):
    T = x2.shape[0]
    TM = TOK_TILE
    nst = S // TM
    tok = lambda w_: pl.BlockSpec((TM, w_), lambda i: (i, 0))
    const = _const_spec
    halo = pl.BlockSpec((POOL_HALO, POOL_W), lambda i: (jnp.maximum(i * (TM // POOL_HALO) - 1, 0), 0))
    return pl.pallas_call(
        functools.partial(_mix_kernel, nst=nst),
        grid=(T // TM,),
        in_specs=[tok(D_MODEL), tok(ATTN_W), tok(POOL_W), halo, const((1, D_MODEL)),
                  const((D_MODEL, 2 * D_MODEL)), const((ATTN_W, D_MODEL)),
                  const((POOL_GROUPS, POOL_DIM, POOL_DIM)), const((1, POOL_W)),
                  const((POOL_W, D_MODEL)), const((D_MODEL, D_MODEL))],
        out_specs=tok(D_MODEL),
        out_shape=jax.ShapeDtypeStruct((T, D_MODEL), F32),
        compiler_params=_params(("parallel",)),
    )(x2, attn, u, u, g1, wm, wap, wpool, ps, wpp, wout)


def _ffn_kernel(x_ref, g2_ref, w1_ref, w2_ref, gf_ref, o_ref, *, final):
    x = x_ref[...]
    h = _rmsnorm(x, g2_ref[...]).astype(BF16)
    acc = x
    for c in range(D_FF // FF_CHUNK):
        cols = slice(c * FF_CHUNK, (c + 1) * FF_CHUNK)
        t = jnp.square(jnp.maximum(_dot(h, w1_ref[:, cols]), 0.0)).astype(BF16)
        acc = acc + _dot(t, w2_ref[cols, :])
    if final:
        acc = _rmsnorm(acc, gf_ref[...])
    o_ref[...] = acc


def _ffn(x2, g2, w1, w2, gf, final):
    T = x2.shape[0]
    TM = TOK_TILE
    tok = pl.BlockSpec((TM, D_MODEL), lambda i: (i, 0))
    const = _const_spec
    return pl.pallas_call(
        functools.partial(_ffn_kernel, final=final),
        grid=(T // TM,),
        in_specs=[tok, const((1, D_MODEL)), const((D_MODEL, D_FF)), const((D_FF, D_MODEL)),
                  const((1, D_MODEL))],
        out_specs=tok,
        out_shape=jax.ShapeDtypeStruct((T, D_MODEL), F32),
        compiler_params=_params(("parallel",)),
    )(x2, g2, w1, w2, gf)


def _pack_w_in(w_in_l):
    scale = HEAD_DIM ** -0.5
    wq = w_in_l[:, :ATTN_W] * scale
    o_kv = ATTN_W
    o_gate = o_kv + 3 * KV_W
    o_pool = o_gate + 3 * N_HEADS
    o_merge = o_pool + POOL_W
    wgate = jnp.pad(w_in_l[:, o_gate:o_pool], ((0, 0), (0, LANES - 3 * N_HEADS)))
    w = jnp.concatenate([wq, w_in_l[:, o_kv:o_gate], wgate, w_in_l[:, o_pool:o_merge]], axis=1)
    return w.astype(BF16), w_in_l[:, o_merge:].astype(BF16)


def _pack_compress(pe_l, w1_l, w2_l):
    eye = jnp.eye(N_KV, dtype=F32)
    half = CMP_BLOCK // 2
    w1r = w1_l.reshape(2, 2, half, HEAD_DIM, CMP_HIDDEN)
    w1big = jnp.einsum('khldm,gG->hlkgdGm', w1r, eye)
    w1big = w1big.reshape(2, half, 2, N_KV * HEAD_DIM, N_KV * CMP_HIDDEN)
    per = pe_l.reshape(2, 2, half, HEAD_DIM)
    pebig = jnp.broadcast_to(per.transpose(1, 2, 0, 3)[:, :, :, None, :],
                             (2, half, 2, N_KV, HEAD_DIM)).reshape(2, half, 2, N_KV * HEAD_DIM)
    w2p = jnp.pad(w2_l, ((0, 0), (0, 0), (0, LANES - HEAD_DIM)))
    w2big = jnp.einsum('kme,kK,gG->kgmKGe', w2p, eye, eye)
    w2big = w2big.reshape(2 * N_KV * CMP_HIDDEN, 2 * N_KV * LANES)
    return pebig, w1big.astype(BF16), w2big.astype(BF16)


def _constants(S):
    NC = S // CMP_STRIDE
    n_sel = S // SEL_BLOCK
    pos = np.arange(S)
    ksc = np.zeros((S, LANES), np.float32)
    ksc[:, HEAD_DIM] = pos % SEL_BLOCK
    blk = pos // SEL_BLOCK
    sel_rows = blk >= 1
    ksc[pos[sel_rows], HEAD_DIM + blk[sel_rows]] = 1.0
    kwc = np.zeros((S, LANES), np.float32)
    kwc[:, HEAD_DIM] = pos % SEL_BLOCK
    kwc[:, HEAD_DIM + 1] = blk
    vc = np.zeros((1, LANES), np.float32)
    vc[0, HEAD_DIM] = 1.0
    qc = np.zeros((1, N_HEADS * LANES), np.float32)
    ssl = np.zeros((N_HEADS, LANES), np.float32)
    for h in range(N_HEADS):
        qc[0, h * LANES + HEAD_DIM] = SLOPES[h]
        qc[0, h * LANES + HEAD_DIM + 1] = SLOPES[h] * SEL_BLOCK
        ssl[h, HEAD_DIM + 2:] = SLOPES[h] * SEL_BLOCK * np.arange(2, SEL_BLOCK)
    c = np.arange(NC)
    cc = np.zeros((NC, LANES), np.float32)
    cstart = c * CMP_STRIDE
    cc[:, HEAD_DIM] = cstart % SEL_BLOCK
    cc[:, HEAD_DIM + 1] = cstart // SEL_BLOCK
    slv = np.repeat(np.asarray(SLOPES, np.float32).reshape(N_KV, 1, GROUP), Q_BLOCK, axis=2)
    j = np.arange(SEL_BLOCK)
    ovt = ((cstart[None, :] <= j[:, None] * SEL_BLOCK + SEL_BLOCK - 1)
           & (cstart[None, :] + CMP_BLOCK - 1 >= j[:, None] * SEL_BLOCK)
           & (j[:, None] < n_sel) & (c[None, :] < NC - 1)).astype(np.float32)
    return (jnp.asarray(qc), jnp.asarray(ksc), jnp.asarray(kwc), jnp.asarray(vc), jnp.asarray(cc),
            jnp.asarray(ovt, BF16), jnp.asarray(ssl), jnp.asarray(slv))


def kernel(x, norm1_g, w_in, cmp_pe, cmp_w1, cmp_w2, w_attn_proj, w_pool, pool_scale,
           w_pool_proj, w_out, norm2_g, w_ff1, w_ff2, final_g):
    B, S, _ = x.shape
    depth = w_in.shape[0]
    assert S % SEL_CHUNK == 0 and S % TOK_TILE == 0 and S // SEL_BLOCK <= SEL_BLOCK and S >= WINDOW + Q_BLOCK
    qc, ksc, kwc, vc, cc, ovt, ssl, slv = _constants(S)
    x2 = x.reshape(B * S, D_MODEL)
    for l in range(depth):
        w_pack, w_merge = _pack_w_in(w_in[l])
        pebig, w1big, w2big = _pack_compress(cmp_pe[l], cmp_w1[l], cmp_w2[l])
        g1 = norm1_g[l].reshape(1, D_MODEL)
        qa, kvc, ksa, vst, kwa, vwt, gates, u = _inproj(x2, g1, w_pack, qc, ksc, kwc, vc, S)
        kca, vct = _compress(kvc, pebig, w1big, w2big, cc, B)
        attn = _nsa(qa, gates, kca, vct, ksa, vst, kwa, vwt, ovt, ssl, slv, B, S)
        x2 = _mix(x2, attn, u, g1, w_merge, w_attn_proj[l].astype(BF16), w_pool[l].astype(BF16),
                  pool_scale[l].reshape(1, POOL_W), w_pool_proj[l].astype(BF16),
                  w_out[l].astype(BF16), S)
        x2 = _ffn(x2, norm2_g[l].reshape(1, D_MODEL), w_ff1[l].astype(BF16), w_ff2[l].astype(BF16),
                  final_g.reshape(1, D_MODEL), final=(l == depth - 1))
    return x2.reshape(B, S, D_MODEL)
```

```python
import functools

import numpy as np
import jax
import jax.numpy as jnp
from jax import lax
from jax.experimental import pallas as pl
from jax.experimental.pallas import tpu as pltpu

F32 = jnp.float32
BF16 = jnp.bfloat16

D_MODEL = 1024
N_HEADS = 8
HEAD_DIM = 64
N_KV = 2
GROUP = N_HEADS // N_KV
ATTN_W = N_HEADS * HEAD_DIM
KV_W = 2 * N_KV * HEAD_DIM
CMP_BLOCK = 32
CMP_STRIDE = 16
CMP_HIDDEN = 64
SEL_BLOCK = 64
SEL_TOP_N = 16
WINDOW = 512
POOL_GROUPS = 4
POOL_DIM = 128
POOL_W = POOL_GROUPS * POOL_DIM
POOL_WINDOWS = (2, 4, 8, 16)
POOL_HALO = 16
D_FF = 4 * D_MODEL
EPS = 1e-6
NEG = -1e30
FORCE = 1e6
UNSEL = -1e9
SLOPES = tuple(2.0 ** (-8.0 * (h + 1) / N_HEADS) for h in range(N_HEADS))

LANES = 128
SUBLANES = 8
Q_BLOCK = 256
SEL_CHUNK = 512
TOK_TILE = 1024
FF_CHUNK = 1024
VMEM_LIMIT = 56 * 1024 * 1024
SHIFT_SLACK = 12.0
RANGE_SLACK = 12.0

_C_Q = 0
_C_KC = _C_Q + ATTN_W
_C_KS = _C_KC + KV_W
_C_VS = _C_KS + LANES
_C_KW = _C_VS + LANES
_C_VW = _C_KW + LANES
_C_GATE = _C_VW + LANES
_C_U = _C_GATE + LANES
_C_END = _C_U + POOL_W


def _dot(a, b):
    return jnp.dot(a, b, preferred_element_type=F32)


def _dot_nt(a, b):
    return lax.dot_general(a, b, (((1,), (1,)), ((), ())), preferred_element_type=F32)


def _rmsnorm(x, g):
    return x * lax.rsqrt(jnp.mean(x * x, axis=-1, keepdims=True) + EPS) * g


def _const_spec(shape):
    return pl.BlockSpec(shape, lambda i: (0,) * len(shape), pipeline_mode=pl.Buffered(1))


def _params(sem):
    return pltpu.CompilerParams(dimension_semantics=sem, vmem_limit_bytes=VMEM_LIMIT)


def _inproj_kernel(x_ref, g_ref, w_ref, qc_ref, ksc_ref, kwc_ref, vc_ref,
                   qa_ref, kvc_ref, ksa_ref, vst_ref, kwa_ref, vwt_ref, gate_ref, u_ref):
    x = x_ref[...]
    h = _rmsnorm(x, g_ref[...]).astype(BF16)

    r_all = _dot(h, w_ref[...])

    def proj(a, b):
        return r_all[:, a:b]

    kvc_ref[0] = proj(_C_KC, _C_KC + LANES)
    kvc_ref[1] = proj(_C_KC + LANES, _C_KS)
    low = lax.broadcasted_iota(jnp.int32, (x.shape[0], LANES), 1) < HEAD_DIM

    def split(r, c0, c1):
        return jnp.where(low, r, c0), jnp.where(low, pltpu.roll(r, HEAD_DIM, 1), c1)

    rq = proj(_C_Q, _C_KC)
    for j in range(N_HEADS // 2):
        lanes = [slice((2 * j + i) * LANES, (2 * j + i + 1) * LANES) for i in range(2)]
        q0, q1 = split(rq[:, j * LANES:(j + 1) * LANES], qc_ref[:, lanes[0]], qc_ref[:, lanes[1]])
        qa_ref[:, lanes[0]] = q0.astype(BF16)
        qa_ref[:, lanes[1]] = q1.astype(BF16)
    for k_ref, v_ref, kc_ref, c0 in ((ksa_ref, vst_ref, ksc_ref, _C_KS), (kwa_ref, vwt_ref, kwc_ref, _C_KW)):
        k0, k1 = split(proj(c0, c0 + LANES), kc_ref[...], kc_ref[...])
        k_ref[0] = k0.astype(BF16)
        k_ref[1] = k1.astype(BF16)
        v0, v1 = split(proj(c0 + LANES, c0 + 2 * LANES), vc_ref[...], vc_ref[...])
        v_ref[0] = v0.T.astype(BF16)
        v_ref[1] = v1.T.astype(BF16)
    gate_ref[...] = jax.nn.sigmoid(proj(_C_GATE, _C_U))
    u_ref[...] = proj(_C_U, _C_END)


def _inproj(x2, g1, w, qc, ksc, kwc, vc, S):
    T = x2.shape[0]
    TM = TOK_TILE
    nst = S // TM
    tok = lambda w_: pl.BlockSpec((TM, w_), lambda i: (i, 0))
    const = _const_spec
    seq = pl.BlockSpec((TM, LANES), lambda i: (i % nst, 0))
    grp = pl.BlockSpec((N_KV, TM, LANES), lambda i: (0, i, 0))
    grp_t = pl.BlockSpec((N_KV, LANES, TM), lambda i: (0, 0, i))
    return pl.pallas_call(
        _inproj_kernel,
        grid=(T // TM,),
        in_specs=[tok(D_MODEL), const((1, D_MODEL)), const((D_MODEL, _C_END)),
                  const((1, N_HEADS * LANES)), seq, seq, const((1, LANES))],
        out_specs=[tok(N_HEADS * LANES), pl.BlockSpec((2, TM, LANES), lambda i: (0, i, 0)),
                   grp, grp_t, grp, grp_t, tok(LANES), tok(POOL_W)],
        out_shape=[jax.ShapeDtypeStruct((T, N_HEADS * LANES), BF16),
                   jax.ShapeDtypeStruct((2, T, LANES), F32),
                   jax.ShapeDtypeStruct((N_KV, T, LANES), BF16),
                   jax.ShapeDtypeStruct((N_KV, LANES, T), BF16),
                   jax.ShapeDtypeStruct((N_KV, T, LANES), BF16),
                   jax.ShapeDtypeStruct((N_KV, LANES, T), BF16),
                   jax.ShapeDtypeStruct((T, LANES), F32),
                   jax.ShapeDtypeStruct((T, POOL_W), F32)],
        compiler_params=_params(("parallel",)),
    )(x2, g1, w, qc, ksc, kwc, vc)


def _compress_kernel(x_ref, pe_ref, w1_ref, w2_ref, cc_ref, kca_ref, vct_ref):
    nc = x_ref.shape[1] // CMP_STRIDE
    hs = []
    for kv in range(2):
        a = jnp.zeros((nc, LANES), F32)
        b = jnp.zeros((nc, LANES), F32)
        for l in range(CMP_STRIDE):
            xl = x_ref[kv, pl.ds(l, nc, stride=CMP_STRIDE), :]
            a = a + _dot((xl + pe_ref[0, l, kv:kv + 1, :]).astype(BF16), w1_ref[0, l, kv])
            b = b + _dot((xl + pe_ref[1, l, kv:kv + 1, :]).astype(BF16), w1_ref[1, l, kv])
        hs.append(a + pltpu.roll(b, nc - 1, 0))
    h = jnp.concatenate(hs, axis=1)
    h = jax.nn.gelu(h).astype(BF16)
    out = _dot(h, w2_ref[...])
    cc = cc_ref[...]
    for g in range(N_KV):
        kca_ref[0, g] = (out[:, g * LANES:(g + 1) * LANES] + cc).astype(BF16)
        vct_ref[0, g] = out[:, (N_KV + g) * LANES:(N_KV + g + 1) * LANES].T.astype(BF16)


def _compress(kvc3, pe, w1, w2, cc, B):
    S = kvc3.shape[1] // B
    NC = S // CMP_STRIDE
    const = lambda shape: pl.BlockSpec(shape, lambda b: (0,) * len(shape))
    return pl.pallas_call(
        _compress_kernel,
        grid=(B,),
        in_specs=[pl.BlockSpec((2, S, LANES), lambda b: (0, b, 0)), const((2, CMP_STRIDE, 2, LANES)),
                  const((2, CMP_STRIDE, 2, LANES, LANES)),
                  const((2 * N_KV * CMP_HIDDEN, 2 * N_KV * LANES)), const((NC, LANES))],
        out_specs=[pl.BlockSpec((1, N_KV, NC, LANES), lambda b: (b, 0, 0, 0)),
                   pl.BlockSpec((1, N_KV, LANES, NC), lambda b: (b, 0, 0, 0))],
        out_shape=[jax.ShapeDtypeStruct((B, N_KV, NC, LANES), BF16),
                   jax.ShapeDtypeStruct((B, N_KV, LANES, NC), BF16)],
        compiler_params=_params(("parallel",)),
    )(kvc3, pe, w1, w2, cc)


def _nsa_kernel(qa_ref, gate_ref, kca_ref, vct_ref, ksa_ref, vst_ref, kwa_ref, vwt_ref,
                ov_ref, ssl_ref, slv_ref, o_ref, qsel_ref, m_ref, acc_ref, pp_ref, pa_ref, s0a_ref, s0b_ref,
                mx_ref, sh_ref, ex_ref, wx_ref, ow_ref, *, S):
    QB, R, M = Q_BLOCK, GROUP, GROUP * Q_BLOCK
    NC = S // CMP_STRIDE
    n_sel = S // SEL_BLOCK
    top_n = min(SEL_TOP_N, n_sel)
    WK = WINDOW + QB
    q0 = pl.program_id(1) * QB
    s0_refs = (s0a_ref, s0b_ref)

    def heads(a):
        return jnp.concatenate([a] * R, axis=1)

    ce = lax.broadcasted_iota(jnp.int32, (NC, QB), 0) * CMP_STRIDE + (CMP_BLOCK - 1)
    vis = (q0 + lax.broadcasted_iota(jnp.int32, (NC, QB), 1)) >= ce
    vbias = heads(jnp.where(vis, 0.0, NEG))
    any_vis = heads((q0 + lax.broadcasted_iota(jnp.int32, (1, QB), 1)) >= CMP_BLOCK - 1)

    jrow = lax.broadcasted_iota(jnp.int32, (SEL_BLOCK, QB), 0)
    cur = (q0 + lax.broadcasted_iota(jnp.int32, (SEL_BLOCK, QB), 1)) // SEL_BLOCK
    future = jrow > cur
    forced = (jrow == 0) | (jrow == cur) | (jrow == cur - 1)
    sub8 = lax.broadcasted_iota(jnp.int32, (SUBLANES, QB), 0)

    ws = pl.multiple_of(jnp.maximum(q0 - WINDOW, 0), QB)

    qas = [jnp.concatenate([qa_ref[:, (g * R + r) * LANES:(g * R + r + 1) * LANES] for r in range(R)],
                           axis=0) for g in range(N_KV)]
    s_cs = [_dot_nt(kca_ref[0, g], qas[g]) for g in range(N_KV)]

    o_cs, imps = [], []
    for g in range(N_KV):
        s = s_cs[g] + vbias
        m = jnp.max(s, axis=0, keepdims=True)
        e = jnp.exp(s - m)
        l = jnp.sum(e, axis=0, keepdims=True)
        p = e * jnp.where(any_vis, 1.0 / l, 0.0)
        o_cs.append(_dot(vct_ref[0, g], p.astype(BF16)))
        psum = p[:, 0:QB] + p[:, QB:2 * QB] + p[:, 2 * QB:3 * QB] + p[:, 3 * QB:4 * QB]
        hi = psum.astype(BF16)
        r1 = psum - hi.astype(F32)
        mid = r1.astype(BF16)
        lo = (r1 - mid.astype(F32)).astype(BF16)
        ov = ov_ref[...]
        imps.append(_dot(ov, hi) + _dot(ov, mid) + _dot(ov, lo))

    def select(g):
        v = jnp.where(future, NEG, jnp.where(forced, FORCE, imps[g]))
        vt = [v[k * SUBLANES:(k + 1) * SUBLANES] for k in range(SEL_BLOCK // SUBLANES)]
        cnt = [jnp.zeros((SUBLANES, QB), jnp.int32) for _ in vt]
        for i in range(n_sel):
            vi = v[i:i + 1, :]
            for k, vk in enumerate(vt):
                if (k + 1) * SUBLANES <= i + 1:
                    beats = jnp.where(vi > vk, 1, 0)
                elif k * SUBLANES > i:
                    beats = jnp.where(vi >= vk, 1, 0)
                else:
                    beats = jnp.where(sub8 > i - k * SUBLANES, jnp.where(vi >= vk, 1, 0),
                                      jnp.where(vi > vk, 1, 0))
                cnt[k] = cnt[k] + beats
        cnt = jnp.concatenate(cnt, axis=0)
        selb = jnp.where(cnt < top_n, jnp.where(v > NEG / 2, 0.0, UNSEL), UNSEL)
        aug = jnp.concatenate([jnp.zeros_like(selb), selb], axis=0).T
        for r in range(R):
            qh = qas[g][r * QB:(r + 1) * QB].astype(F32)
            qsel_ref[g, r * QB:(r + 1) * QB, :] = (
                qh + aug + ssl_ref[g * R + r:g * R + r + 1, :]).astype(BF16)

    def window(g, streaming):
        def positions(n, k0):
            return (q0 + lax.broadcasted_iota(jnp.int32, (n, QB), 1),
                    k0 + lax.broadcasted_iota(jnp.int32, (n, QB), 0))

        if not streaming:
            tq, kpos = positions(WK, ws)
            dw = tq - kpos
            sw = (_dot_nt(kwa_ref[g, pl.ds(ws, WK), :], qas[g])
                  + heads(jnp.where(dw >= 0, jnp.where(dw < WINDOW, 0.0, NEG), NEG)))
            pw = jnp.exp((sw - jnp.max(sw, axis=0, keepdims=True)).astype(BF16))
            accw = _dot(vwt_ref[g, :, pl.ds(ws, WK)], pw)
        else:
            qs = pl.multiple_of(q0, QB)
            tq, kd = positions(QB, qs)
            sd = (_dot_nt(kwa_ref[g, pl.ds(qs, QB), :], qas[g])
                  + heads(jnp.where(kd <= tq, 0.0, NEG)))
            md = jnp.max(sd, axis=0, keepdims=True)
            pd = jnp.exp((sd - md).astype(BF16))
            tq, kpos = positions(WINDOW, ws)
            far = jnp.where(kpos < q0, jnp.where(tq - kpos < WINDOW, 0.0, NEG), NEG)
            sf = _dot_nt(kwa_ref[g, pl.ds(ws, WINDOW), :], qas[g]) + heads(far)
            wx_ref[g] = jnp.max(sf, axis=0, keepdims=True) - md
            pf = jnp.exp((sf - md).astype(BF16))
            accw = (_dot(vwt_ref[g, :, pl.ds(qs, QB)], pd) + _dot(vwt_ref[g, :, pl.ds(ws, WINDOW)], pf))
        ow_ref[g] = accw * (1.0 / accw[HEAD_DIM:HEAD_DIM + 1, :])

    def sel_scores(g, k0):
        return _dot_nt(ksa_ref[g, pl.ds(k0, SEL_CHUNK), :], qsel_ref[g])

    def seed_max(g):
        kpos0 = lax.broadcasted_iota(jnp.int32, (SEL_BLOCK, QB), 0)
        tq0 = q0 + lax.broadcasted_iota(jnp.int32, (SEL_BLOCK, QB), 1)
        t0 = _dot_nt(ksa_ref[g, 0:SEL_BLOCK, :], qsel_ref[g]) + heads(jnp.where(kpos0 <= tq0, 0.0, NEG))
        return jnp.max(t0, axis=0, keepdims=True)

    select(0)
    seeds = [seed_max(0)]
    s0_refs[0][...] = sel_scores(0, 0)
    window(0, True)
    select(1)
    seeds.append(seed_max(1))
    window(1, True)

    tvec = heads(q0 + lax.broadcasted_iota(jnp.int32, (1, QB), 1)).astype(F32)

    def bias_cap(g, k_hi):
        return slv_ref[g] * jnp.minimum(tvec, k_hi)

    def pending_pv(k0):
        acc_ref[1] = pa_ref[...] * acc_ref[1] + _dot(vst_ref[1, :, pl.ds(k0, SEL_CHUNK)], pp_ref[...])

    nfull = q0 // SEL_CHUNK

    def run_selected(streaming):
        acc_ref[...] = jnp.zeros(acc_ref.shape, F32)
        pp_ref[...] = jnp.zeros(pp_ref.shape, BF16)
        pa_ref[...] = jnp.ones(pa_ref.shape, F32)
        if streaming:
            for g in range(N_KV):
                r0 = seeds[g] - bias_cap(g, SEL_BLOCK - 1.0)
                m_ref[g] = r0
                mx_ref[g] = seeds[g]
                sh_ref[g] = r0 + bias_cap(g, SEL_CHUNK - 1.0)
            ex_ref[...] = jnp.zeros(ex_ref.shape, F32)
        else:
            m_ref[...] = jnp.full(m_ref.shape, NEG, F32)
            s0_refs[0][...] = sel_scores(0, 0)

        def online(g, sc, k0):
            m_cur = jnp.max(sc, axis=0, keepdims=True)
            m_prev = m_ref[g]
            if not streaming:
                m_new = jnp.maximum(m_prev, m_cur)
                m_ref[g] = m_new
                return jnp.exp((sc - m_new).astype(BF16)), jnp.exp(m_prev - m_new)
            cap = bias_cap(g, (k0 + (SEL_CHUNK - 1)).astype(F32))
            shift = m_prev + cap
            r_cur = m_cur - cap
            m_ref[g] = jnp.maximum(m_prev, r_cur)
            mx_ref[g] = jnp.maximum(mx_ref[g], m_cur)
            ex_ref[g] = jnp.maximum(ex_ref[g], r_cur - m_prev)
            alpha = jnp.exp(sh_ref[g] - shift)
            sh_ref[g] = shift
            return jnp.exp((sc - shift).astype(BF16)), alpha

        def step(c, src_ref, dst_ref):
            last = dst_ref is None
            k0 = pl.multiple_of(c * SEL_CHUNK, SEL_CHUNK)
            kprev = pl.multiple_of(jnp.maximum(c - 1, 0) * SEL_CHUNK, SEL_CHUNK)
            s1 = sel_scores(1, k0)
            pending_pv(kprev)
            if not last:
                dst_ref[...] = sel_scores(0, pl.multiple_of(k0 + SEL_CHUNK, SEL_CHUNK))
            s0 = src_ref[...]
            if last:
                kpos = k0 + lax.broadcasted_iota(jnp.int32, (SEL_CHUNK, QB), 0)
                tq = q0 + lax.broadcasted_iota(jnp.int32, (SEL_CHUNK, QB), 1)
                cb = heads(jnp.where(kpos <= tq, 0.0, NEG))
                s0 = s0 + cb
                s1 = s1 + cb
            p0, alpha0 = online(0, s0, k0)
            acc_ref[0] = alpha0 * acc_ref[0] + _dot(vst_ref[0, :, pl.ds(k0, SEL_CHUNK)], p0)
            p1, alpha1 = online(1, s1, k0)
            pp_ref[...] = p1
            pa_ref[...] = alpha1

        def body(c, carry):
            for par in range(2):
                pl.when(c % 2 == par)(functools.partial(step, c, s0_refs[par], s0_refs[1 - par]))
            return carry

        lax.fori_loop(0, nfull, body, 0)
        for par in range(2):
            pl.when(nfull % 2 == par)(functools.partial(step, nfull, s0_refs[par], None))

    run_selected(True)
    unsafe = jnp.maximum(jnp.maximum(ex_ref[...], wx_ref[...]) - SHIFT_SLACK,
                         sh_ref[...] - mx_ref[...] - RANGE_SLACK)

    @pl.when(jnp.max(unsafe) > 0.0)
    def _():
        for g in range(N_KV):
            window(g, False)
        run_selected(False)

    pending_pv(pl.multiple_of(nfull * SEL_CHUNK, SEL_CHUNK))


    gt = gate_ref[...].T
    outs = []
    for g in range(N_KV):
        acc = acc_ref[g]
        o_s = acc * (1.0 / acc[HEAD_DIM:HEAD_DIM + 1, :])
        o_c, o_w = o_cs[g], ow_ref[g]
        for r in range(R):
            col = (g * R + r) * 3
            cols = slice(r * QB, (r + 1) * QB)
            o = (gt[col:col + 1] * o_c[:HEAD_DIM, cols] + gt[col + 1:col + 2] * o_s[:HEAD_DIM, cols]
                 + gt[col + 2:col + 3] * o_w[:HEAD_DIM, cols])
            outs.append(o)
    o_ref[...] = jnp.concatenate(outs, axis=0).T.astype(BF16)


def _nsa(qa, gates, kca, vct, ksa, vst, kwa, vwt, ov, ssl, slv, B, S):
    NC = S // CMP_STRIDE
    nqb = S // Q_BLOCK
    M = GROUP * Q_BLOCK
    tok = lambda w_: pl.BlockSpec((Q_BLOCK, w_), lambda b, i: (b * nqb + i, 0))
    const = lambda shape: pl.BlockSpec(shape, lambda b, i: (0,) * len(shape))
    seq = pl.BlockSpec((N_KV, S, LANES), lambda b, i: (0, b, 0))
    seq_t = pl.BlockSpec((N_KV, LANES, S), lambda b, i: (0, 0, b))
    return pl.pallas_call(
        functools.partial(_nsa_kernel, S=S),
        grid=(B, nqb),
        in_specs=[tok(N_HEADS * LANES), tok(LANES),
                  pl.BlockSpec((1, N_KV, NC, LANES), lambda b, i: (b, 0, 0, 0)),
                  pl.BlockSpec((1, N_KV, LANES, NC), lambda b, i: (b, 0, 0, 0)),
                  seq, seq_t, seq, seq_t,
                  const((SEL_BLOCK, NC)), const((N_HEADS, LANES)), const((N_KV, 1, M))],
        out_specs=tok(ATTN_W),
        out_shape=jax.ShapeDtypeStruct((B * S, ATTN_W), BF16),
        scratch_shapes=[pltpu.VMEM((N_KV, M, LANES), BF16), pltpu.VMEM((N_KV, 1, M), F32),
                        pltpu.VMEM((N_KV, LANES, M), F32),
                        pltpu.VMEM((SEL_CHUNK, M), BF16), pltpu.VMEM((1, M), F32),
                        pltpu.VMEM((SEL_CHUNK, M), F32), pltpu.VMEM((SEL_CHUNK, M), F32)]
                       + [pltpu.VMEM((N_KV, 1, M), F32)] * 4 + [pltpu.VMEM((N_KV, LANES, M), F32)],
        compiler_params=_params(("parallel", "arbitrary")),
    )(qa, gates, kca, vct, ksa, vst, kwa, vwt, ov, ssl, slv)


def _mix_kernel(x_ref, attn_ref, u_ref, uh_ref, g1_ref, wm_ref, wap_ref, wpool_ref, ps_ref,
                wpp_ref, wout_ref, o_ref, *, nst):
    TM = x_ref.shape[0]
    seq_tile = pl.program_id(0) % nst
    x = x_ref[...]
    h = _rmsnorm(x, g1_ref[...]).astype(BF16)
    merge = jax.nn.sigmoid(_dot(h, wm_ref[...]))
    a = _dot(attn_ref[...], wap_ref[...])

    halo = jnp.where(seq_tile == 0, 0.0, uh_ref[...])
    ue = jnp.concatenate([halo, u_ref[...]], axis=0)
    pos = seq_tile * TM + lax.broadcasted_iota(jnp.int32, (TM, 1), 0)
    ys = []
    for gi, w in enumerate(POOL_WINDOWS):
        ug = ue[:, gi * POOL_DIM:(gi + 1) * POOL_DIM]
        s = ug
        sh = 1
        while sh < w:
            s = s + pltpu.roll(s, sh, 0)
            sh *= 2
        cnt = jnp.minimum(pos + 1, w).astype(F32)
        p = s[POOL_HALO:] / cnt - ug[POOL_HALO:]
        ys.append(_dot(p.astype(BF16), wpool_ref[gi]))
    y = jnp.concatenate(ys, axis=1) * ps_ref[...]
    b = _dot(y.astype(BF16), wpp_ref[...])
    mix = merge[:, :D_MODEL] * a + merge[:, D_MODEL:] * b
    o_ref[...] = x + _dot(mix.astype(BF16), wout_ref[...])


def _mix(x2, attn, u, g1, wm, wap, wpool, ps, wpp, wout, S):
    T = x2.shape[0]
    TM = TOK_TILE
    nst = S // TM
    tok = lambda w_: pl.BlockSpec((TM, w_), lambda i: (i, 0))
    const = _const_spec
    halo = pl.BlockSpec((POOL_HALO, POOL_W), lambda i: (jnp.maximum(i * (TM // POOL_HALO) - 1, 0), 0))
    return pl.pallas_call(
        functools.partial(_mix_kernel, nst=nst),
        grid=(T // TM,),
        in_specs=[tok(D_MODEL), tok(ATTN_W), tok(POOL_W), halo, const((1, D_MODEL)),
                  const((D_MODEL, 2 * D_MODEL)), const((ATTN_W, D_MODEL)),
                  const((POOL_GROUPS, POOL_DIM, POOL_DIM)), const((1, POOL_W)),
                  const((POOL_W, D_MODEL)), const((D_MODEL, D_MODEL))],
        out_specs=tok(D_MODEL),
        out_shape=jax.ShapeDtypeStruct((T, D_MODEL), F32),
        compiler_params=_params(("parallel",)),
    )(x2, attn, u, u, g1, wm, wap, wpool, ps, wpp, wout)


def _ffn_kernel(x_ref, g2_ref, w1_ref, w2_ref, gf_ref, o_ref, *, final):
    x = x_ref[...]
    h = _rmsnorm(x, g2_ref[...]).astype(BF16)
    acc = x
    for c in range(D_FF // FF_CHUNK):
        cols = slice(c * FF_CHUNK, (c + 1) * FF_CHUNK)
        t = jnp.square(jnp.maximum(_dot(h, w1_ref[:, cols]), 0.0)).astype(BF16)
        acc = acc + _dot(t, w2_ref[cols, :])
    if final:
        acc = _rmsnorm(acc, gf_ref[...])
    o_ref[...] = acc


def _ffn(x2, g2, w1, w2, gf, final):
    T = x2.shape[0]
    TM = TOK_TILE
    tok = pl.BlockSpec((TM, D_MODEL), lambda i: (i, 0))
    const = _const_spec
    return pl.pallas_call(
        functools.partial(_ffn_kernel, final=final),
        grid=(T // TM,),
        in_specs=[tok, const((1, D_MODEL)), const((D_MODEL, D_FF)), const((D_FF, D_MODEL)),
                  const((1, D_MODEL))],
        out_specs=tok,
        out_shape=jax.ShapeDtypeStruct((T, D_MODEL), F32),
        compiler_params=_params(("parallel",)),
    )(x2, g2, w1, w2, gf)


def _pack_w_in(w_in_l):
    scale = HEAD_DIM ** -0.5
    wq = w_in_l[:, :ATTN_W] * scale
    o_kv = ATTN_W
    o_gate = o_kv + 3 * KV_W
    o_pool = o_gate + 3 * N_HEADS
    o_merge = o_pool + POOL_W
    wgate = jnp.pad(w_in_l[:, o_gate:o_pool], ((0, 0), (0, LANES - 3 * N_HEADS)))
    w = jnp.concatenate([wq, w_in_l[:, o_kv:o_gate], wgate, w_in_l[:, o_pool:o_merge]], axis=1)
    return w.astype(BF16), w_in_l[:, o_merge:].astype(BF16)


def _pack_compress(pe_l, w1_l, w2_l):
    eye = jnp.eye(N_KV, dtype=F32)
    half = CMP_BLOCK // 2
    w1r = w1_l.reshape(2, 2, half, HEAD_DIM, CMP_HIDDEN)
    w1big = jnp.einsum('khldm,gG->hlkgdGm', w1r, eye)
    w1big = w1big.reshape(2, half, 2, N_KV * HEAD_DIM, N_KV * CMP_HIDDEN)
    per = pe_l.reshape(2, 2, half, HEAD_DIM)
    pebig = jnp.broadcast_to(per.transpose(1, 2, 0, 3)[:, :, :, None, :],
                             (2, half, 2, N_KV, HEAD_DIM)).reshape(2, half, 2, N_KV * HEAD_DIM)
    w2p = jnp.pad(w2_l, ((0, 0), (0, 0), (0, LANES - HEAD_DIM)))
    w2big = jnp.einsum('kme,kK,gG->kgmKGe', w2p, eye, eye)
    w2big = w2big.reshape(2 * N_KV * CMP_HIDDEN, 2 * N_KV * LANES)
    return pebig, w1big.astype(BF16), w2big.astype(BF16)


def _constants(S):
    NC = S // CMP_STRIDE
    n_sel = S // SEL_BLOCK
    pos = np.arange(S)
    ksc = np.zeros((S, LANES), np.float32)
    ksc[:, HEAD_DIM] = pos % SEL_BLOCK
    blk = pos // SEL_BLOCK
    sel_rows = blk >= 1
    ksc[pos[sel_rows], HEAD_DIM + blk[sel_rows]] = 1.0
    kwc = np.zeros((S, LANES), np.float32)
    kwc[:, HEAD_DIM] = pos % SEL_BLOCK
    kwc[:, HEAD_DIM + 1] = blk
    vc = np.zeros((1, LANES), np.float32)
    vc[0, HEAD_DIM] = 1.0
    qc = np.zeros((1, N_HEADS * LANES), np.float32)
    ssl = np.zeros((N_HEADS, LANES), np.float32)
    for h in range(N_HEADS):
        qc[0, h * LANES + HEAD_DIM] = SLOPES[h]
        qc[0, h * LANES + HEAD_DIM + 1] = SLOPES[h] * SEL_BLOCK
        ssl[h, HEAD_DIM + 2:] = SLOPES[h] * SEL_BLOCK * np.arange(2, SEL_BLOCK)
    c = np.arange(NC)
    cc = np.zeros((NC, LANES), np.float32)
    cstart = c * CMP_STRIDE
    cc[:, HEAD_DIM] = cstart % SEL_BLOCK
    cc[:, HEAD_DIM + 1] = cstart // SEL_BLOCK
    slv = np.repeat(np.asarray(SLOPES, np.float32).reshape(N_KV, 1, GROUP), Q_BLOCK, axis=2)
    j = np.arange(SEL_BLOCK)
    ovt = ((cstart[None, :] <= j[:, None] * SEL_BLOCK + SEL_BLOCK - 1)
           & (cstart[None, :] + CMP_BLOCK - 1 >= j[:, None] * SEL_BLOCK)
           & (j[:, None] < n_sel) & (c[None, :] < NC - 1)).astype(np.float32)
    return (jnp.asarray(qc), jnp.asarray(ksc), jnp.asarray(kwc), jnp.asarray(vc), jnp.asarray(cc),
            jnp.asarray(ovt, BF16), jnp.asarray(ssl), jnp.asarray(slv))


def kernel(x, norm1_g, w_in, cmp_pe, cmp_w1, cmp_w2, w_attn_proj, w_pool, pool_scale,
           w_pool_proj, w_out, norm2_g, w_ff1, w_ff2, final_g):
    B, S, _ = x.shape
    depth = w_in.shape[0]
    assert S % SEL_CHUNK == 0 and S % TOK_TILE == 0 and S // SEL_BLOCK <= SEL_BLOCK and S >= WINDOW + Q_BLOCK
    qc, ksc, kwc, vc, cc, ovt, ssl, slv = _constants(S)
    x2 = x.reshape(B * S, D_MODEL)
    for l in range(depth):
        w_pack, w_merge = _pack_w_in(w_in[l])
        pebig, w1big, w2big = _pack_compress(cmp_pe[l], cmp_w1[l], cmp_w2[l])
        g1 = norm1_g[l].reshape(1, D_MODEL)
        qa, kvc, ksa, vst, kwa, vwt, gates, u = _inproj(x2, g1, w_pack, qc, ksc, kwc, vc, S)
        kca, vct = _compress(kvc, pebig, w1big, w2big, cc, B)
        attn = _nsa(qa, gates, kca, vct, ksa, vst, kwa, vwt, ovt, ssl, slv, B, S)
        x2 = _mix(x2, attn, u, g1, w_merge, w_attn_proj[l].astype(BF16), w_pool[l].astype(BF16),
                  pool_scale[l].reshape(1, POOL_W), w_pool_proj[l].astype(BF16),
                  w_out[l].astype(BF16), S)
        x2 = _ffn(x2, norm2_g[l].reshape(1, D_MODEL), w_ff1[l].astype(BF16), w_ff2[l].astype(BF16),
                  final_g.reshape(1, D_MODEL), final=(l == depth - 1))
    return x2.reshape(B, S, D_MODEL)
```

```python
import functools

import numpy as np
import jax
import jax.numpy as jnp
from jax import lax
from jax.experimental import pallas as pl
from jax.experimental.pallas import tpu as pltpu

F32 = jnp.float32
BF16 = jnp.bfloat16

D_MODEL = 1024
N_HEADS = 8
HEAD_DIM = 64
N_KV = 2
GROUP = N_HEADS // N_KV
ATTN_W = N_HEADS * HEAD_DIM
KV_W = 2 * N_KV * HEAD_DIM
CMP_BLOCK = 32
CMP_STRIDE = 16
CMP_HIDDEN = 64
SEL_BLOCK = 64
SEL_TOP_N = 16
WINDOW = 512
POOL_GROUPS = 4
POOL_DIM = 128
POOL_W = POOL_GROUPS * POOL_DIM
POOL_WINDOWS = (2, 4, 8, 16)
POOL_HALO = 16
D_FF = 4 * D_MODEL
EPS = 1e-6
NEG = -1e30
FORCE = 1e6
UNSEL = -1e9
SLOPES = tuple(2.0 ** (-8.0 * (h + 1) / N_HEADS) for h in range(N_HEADS))

LANES = 128
SUBLANES = 8
Q_BLOCK = 256
SEL_CHUNK = 512
TOK_TILE = 1024
FF_CHUNK = 1024
VMEM_LIMIT = 56 * 1024 * 1024
SHIFT_SLACK = 12.0
RANGE_SLACK = 12.0

_C_Q = 0
_C_KC = _C_Q + ATTN_W
_C_KS = _C_KC + KV_W
_C_VS = _C_KS + LANES
_C_KW = _C_VS + LANES
_C_VW = _C_KW + LANES
_C_GATE = _C_VW + LANES
_C_U = _C_GATE + LANES
_C_END = _C_U + POOL_W


def _dot(a, b):
    return jnp.dot(a, b, preferred_element_type=F32)


def _dot_nt(a, b):
    return lax.dot_general(a, b, (((1,), (1,)), ((), ())), preferred_element_type=F32)


def _rmsnorm(x, g):
    return x * lax.rsqrt(jnp.mean(x * x, axis=-1, keepdims=True) + EPS) * g


def _const_spec(shape):
    return pl.BlockSpec(shape, lambda i: (0,) * len(shape), pipeline_mode=pl.Buffered(1))


def _params(sem):
    return pltpu.CompilerParams(dimension_semantics=sem, vmem_limit_bytes=VMEM_LIMIT)


def _inproj_kernel(x_ref, g_ref, w_ref, qc_ref, ksc_ref, kwc_ref, vc_ref,
                   qa_ref, kvc_ref, ksa_ref, vst_ref, kwa_ref, vwt_ref, gate_ref, u_ref):
    x = x_ref[...]
    h = _rmsnorm(x, g_ref[...]).astype(BF16)

    r_all = _dot(h, w_ref[...])

    def proj(a, b):
        return r_all[:, a:b]

    kvc_ref[0] = proj(_C_KC, _C_KC + LANES)
    kvc_ref[1] = proj(_C_KC + LANES, _C_KS)
    low = lax.broadcasted_iota(jnp.int32, (x.shape[0], LANES), 1) < HEAD_DIM

    def split(r, c0, c1):
        return jnp.where(low, r, c0), jnp.where(low, pltpu.roll(r, HEAD_DIM, 1), c1)

    rq = proj(_C_Q, _C_KC)
    for j in range(N_HEADS // 2):
        lanes = [slice((2 * j + i) * LANES, (2 * j + i + 1) * LANES) for i in range(2)]
        q0, q1 = split(rq[:, j * LANES:(j + 1) * LANES], qc_ref[:, lanes[0]], qc_ref[:, lanes[1]])
        qa_ref[:, lanes[0]] = q0.astype(BF16)
        qa_ref[:, lanes[1]] = q1.astype(BF16)
    for k_ref, v_ref, kc_ref, c0 in ((ksa_ref, vst_ref, ksc_ref, _C_KS), (kwa_ref, vwt_ref, kwc_ref, _C_KW)):
        k0, k1 = split(proj(c0, c0 + LANES), kc_ref[...], kc_ref[...])
        k_ref[0] = k0.astype(BF16)
        k_ref[1] = k1.astype(BF16)
        v0, v1 = split(proj(c0 + LANES, c0 + 2 * LANES), vc_ref[...], vc_ref[...])
        v_ref[0] = v0.T.astype(BF16)
        v_ref[1] = v1.T.astype(BF16)
    gate_ref[...] = jax.nn.sigmoid(proj(_C_GATE, _C_U))
    u_ref[...] = proj(_C_U, _C_END)


def _inproj(x2, g1, w, qc, ksc, kwc, vc, S):
    T = x2.shape[0]
    TM = TOK_TILE
    nst = S // TM
    tok = lambda w_: pl.BlockSpec((TM, w_), lambda i: (i, 0))
    const = _const_spec
    seq = pl.BlockSpec((TM, LANES), lambda i: (i % nst, 0))
    grp = pl.BlockSpec((N_KV, TM, LANES), lambda i: (0, i, 0))
    grp_t = pl.BlockSpec((N_KV, LANES, TM), lambda i: (0, 0, i))
    return pl.pallas_call(
        _inproj_kernel,
        grid=(T // TM,),
        in_specs=[tok(D_MODEL), const((1, D_MODEL)), const((D_MODEL, _C_END)),
                  const((1, N_HEADS * LANES)), seq, seq, const((1, LANES))],
        out_specs=[tok(N_HEADS * LANES), pl.BlockSpec((2, TM, LANES), lambda i: (0, i, 0)),
                   grp, grp_t, grp, grp_t, tok(LANES), tok(POOL_W)],
        out_shape=[jax.ShapeDtypeStruct((T, N_HEADS * LANES), BF16),
                   jax.ShapeDtypeStruct((2, T, LANES), F32),
                   jax.ShapeDtypeStruct((N_KV, T, LANES), BF16),
                   jax.ShapeDtypeStruct((N_KV, LANES, T), BF16),
                   jax.ShapeDtypeStruct((N_KV, T, LANES), BF16),
                   jax.ShapeDtypeStruct((N_KV, LANES, T), BF16),
                   jax.ShapeDtypeStruct((T, LANES), F32),
                   jax.ShapeDtypeStruct((T, POOL_W), F32)],
        compiler_params=_params(("parallel",)),
    )(x2, g1, w, qc, ksc, kwc, vc)


def _compress_kernel(x_ref, pe_ref, w1_ref, w2_ref, cc_ref, kca_ref, vct_ref):
    nc = x_ref.shape[1] // CMP_STRIDE
    hs = []
    for kv in range(2):
        a = jnp.zeros((nc, LANES), F32)
        b = jnp.zeros((nc, LANES), F32)
        for l in range(CMP_STRIDE):
            xl = x_ref[kv, pl.ds(l, nc, stride=CMP_STRIDE), :]
            a = a + _dot((xl + pe_ref[0, l, kv:kv + 1, :]).astype(BF16), w1_ref[0, l, kv])
            b = b + _dot((xl + pe_ref[1, l, kv:kv + 1, :]).astype(BF16), w1_ref[1, l, kv])
        hs.append(a + pltpu.roll(b, nc - 1, 0))
    h = jnp.concatenate(hs, axis=1)
    h = jax.nn.gelu(h).astype(BF16)
    out = _dot(h, w2_ref[...])
    cc = cc_ref[...]
    for g in range(N_KV):
        kca_ref[0, g] = (out[:, g * LANES:(g + 1) * LANES] + cc).astype(BF16)
        vct_ref[0, g] = out[:, (N_KV + g) * LANES:(N_KV + g + 1) * LANES].T.astype(BF16)


def _compress(kvc3, pe, w1, w2, cc, B):
    S = kvc3.shape[1] // B
    NC = S // CMP_STRIDE
    const = lambda shape: pl.BlockSpec(shape, lambda b: (0,) * len(shape))
    return pl.pallas_call(
        _compress_kernel,
        grid=(B,),
        in_specs=[pl.BlockSpec((2, S, LANES), lambda b: (0, b, 0)), const((2, CMP_STRIDE, 2, LANES)),
                  const((2, CMP_STRIDE, 2, LANES, LANES)),
                  const((2 * N_KV * CMP_HIDDEN, 2 * N_KV * LANES)), const((NC, LANES))],
        out_specs=[pl.BlockSpec((1, N_KV, NC, LANES), lambda b: (b, 0, 0, 0)),
                   pl.BlockSpec((1, N_KV, LANES, NC), lambda b: (b, 0, 0, 0))],
        out_shape=[jax.ShapeDtypeStruct((B, N_KV, NC, LANES), BF16),
                   jax.ShapeDtypeStruct((B, N_KV, LANES, NC), BF16)],
        compiler_params=_params(("parallel",)),
    )(kvc3, pe, w1, w2, cc)


def _nsa_kernel(qa_ref, gate_ref, kca_ref, vct_ref, ksa_ref, vst_ref, kwa_ref, vwt_ref,
                ov_ref, ssl_ref, slv_ref, seedm_ref, farm_ref, cbm_ref, o_ref, qsel_ref, m_ref, acc_ref, pp_ref, pa_ref, s0a_ref, s0b_ref,
                mx_ref, sh_ref, ex_ref, wx_ref, ow_ref, *, S):
    QB, R, M = Q_BLOCK, GROUP, GROUP * Q_BLOCK
    NC = S // CMP_STRIDE
    n_sel = S // SEL_BLOCK
    top_n = min(SEL_TOP_N, n_sel)
    WK = WINDOW + QB
    q0 = pl.program_id(1) * QB
    s0_refs = (s0a_ref, s0b_ref)

    def heads(a):
        return jnp.concatenate([a] * R, axis=1)

    def add_heads(x, a):
        return jnp.concatenate([x[:, r * QB:(r + 1) * QB] + a for r in range(R)], axis=1)

    ce = lax.broadcasted_iota(jnp.int32, (NC, QB), 0) * CMP_STRIDE + (CMP_BLOCK - 1)
    vis = (q0 + lax.broadcasted_iota(jnp.int32, (NC, QB), 1)) >= ce
    vbias = jnp.where(vis, 0.0, NEG)
    any_vis = heads((q0 + lax.broadcasted_iota(jnp.int32, (1, QB), 1)) >= CMP_BLOCK - 1)

    jrow = lax.broadcasted_iota(jnp.int32, (SEL_BLOCK, QB), 0)
    cur = (q0 + lax.broadcasted_iota(jnp.int32, (SEL_BLOCK, QB), 1)) // SEL_BLOCK
    future = jrow > cur
    forced = (jrow == 0) | (jrow == cur) | (jrow == cur - 1)
    sub8 = lax.broadcasted_iota(jnp.int32, (SUBLANES, QB), 0)

    ws = pl.multiple_of(jnp.maximum(q0 - WINDOW, 0), QB)

    qas = [jnp.concatenate([qa_ref[:, (g * R + r) * LANES:(g * R + r + 1) * LANES] for r in range(R)],
                           axis=0) for g in range(N_KV)]
    s_cs = [_dot_nt(kca_ref[0, g], qas[g]) for g in range(N_KV)]

    o_cs, imps = [], []
    for g in range(N_KV):
        s = add_heads(s_cs[g], vbias)
        m = jnp.max(s, axis=0, keepdims=True)
        e = jnp.exp(s - m)
        l = jnp.sum(e, axis=0, keepdims=True)
        inv = jnp.where(any_vis, 1.0 / l, 0.0)
        o_cs.append(_dot(vct_ref[0, g], e.astype(BF16)) * inv)
        psum = sum(e[:, r * QB:(r + 1) * QB] * inv[:, r * QB:(r + 1) * QB] for r in range(R))
        hi = psum.astype(BF16)
        r1 = psum - hi.astype(F32)
        mid = r1.astype(BF16)
        lo = (r1 - mid.astype(F32)).astype(BF16)
        ov = ov_ref[...]
        imps.append(_dot(ov, hi) + _dot(ov, mid) + _dot(ov, lo))

    def select(g):
        v = jnp.where(future, NEG, jnp.where(forced, FORCE, imps[g]))
        vt = [v[k * SUBLANES:(k + 1) * SUBLANES] for k in range(SEL_BLOCK // SUBLANES)]
        cnt = [jnp.zeros((SUBLANES, QB), jnp.int32) for _ in vt]
        for i in range(n_sel):
            vi = v[i:i + 1, :]
            for k, vk in enumerate(vt):
                if (k + 1) * SUBLANES <= i + 1:
                    beats = jnp.where(vi > vk, 1, 0)
                elif k * SUBLANES > i:
                    beats = jnp.where(vi >= vk, 1, 0)
                else:
                    beats = jnp.where(sub8 > i - k * SUBLANES, jnp.where(vi >= vk, 1, 0),
                                      jnp.where(vi > vk, 1, 0))
                cnt[k] = cnt[k] + beats
        cnt = jnp.concatenate(cnt, axis=0)
        selb = jnp.where(cnt < top_n, jnp.where(v > NEG / 2, 0.0, UNSEL), UNSEL)
        aug = jnp.concatenate([jnp.zeros_like(selb), selb], axis=0).T
        for r in range(R):
            qh = qas[g][r * QB:(r + 1) * QB].astype(F32)
            qsel_ref[g, r * QB:(r + 1) * QB, :] = (
                qh + aug + ssl_ref[g * R + r:g * R + r + 1, :]).astype(BF16)

    def window(g, streaming):
        def positions(n, k0):
            return (q0 + lax.broadcasted_iota(jnp.int32, (n, QB), 1),
                    k0 + lax.broadcasted_iota(jnp.int32, (n, QB), 0))

        if not streaming:
            tq, kpos = positions(WK, ws)
            dw = tq - kpos
            sw = add_heads(_dot_nt(kwa_ref[g, pl.ds(ws, WK), :], qas[g]),
                           jnp.where(dw >= 0, jnp.where(dw < WINDOW, 0.0, NEG), NEG))
            pw = jnp.exp((sw - jnp.max(sw, axis=0, keepdims=True)).astype(BF16))
            accw = _dot(vwt_ref[g, :, pl.ds(ws, WK)], pw)
        else:
            qs = pl.multiple_of(q0, QB)
            sd = add_heads(_dot_nt(kwa_ref[g, pl.ds(qs, QB), :], qas[g]), seedm_ref[...])
            md = jnp.max(sd, axis=0, keepdims=True)
            pd = jnp.exp((sd - md).astype(BF16))
            far = farm_ref[jnp.minimum(q0 // QB, WINDOW // QB)]
            sf = add_heads(_dot_nt(kwa_ref[g, pl.ds(ws, WINDOW), :], qas[g]), far)
            wx_ref[g] = jnp.max(sf, axis=0, keepdims=True) - md
            pf = jnp.exp((sf - md).astype(BF16))
            accw = (_dot(vwt_ref[g, :, pl.ds(qs, QB)], pd) + _dot(vwt_ref[g, :, pl.ds(ws, WINDOW)], pf))
        ow_ref[g] = accw * (1.0 / accw[HEAD_DIM:HEAD_DIM + 1, :])

    def sel_scores(g, k0):
        return _dot_nt(ksa_ref[g, pl.ds(k0, SEL_CHUNK), :], qsel_ref[g])

    def seed_max(g):
        kpos0 = lax.broadcasted_iota(jnp.int32, (SEL_BLOCK, QB), 0)
        tq0 = q0 + lax.broadcasted_iota(jnp.int32, (SEL_BLOCK, QB), 1)
        t0 = add_heads(_dot_nt(ksa_ref[g, 0:SEL_BLOCK, :], qsel_ref[g]), jnp.where(kpos0 <= tq0, 0.0, NEG))
        return jnp.max(t0, axis=0, keepdims=True)

    select(0)
    window(0, True)
    select(1)
    window(1, True)

    tvec = heads(q0 + lax.broadcasted_iota(jnp.int32, (1, QB), 1)).astype(F32)

    def bias_cap(g, k_hi):
        return slv_ref[g] * jnp.minimum(tvec, k_hi)

    def pending_pv(k0):
        acc_ref[1] = pa_ref[...] * acc_ref[1] + _dot(vst_ref[1, :, pl.ds(k0, SEL_CHUNK)], pp_ref[...])

    nfull = q0 // SEL_CHUNK

    def run_selected(streaming):
        acc_ref[...] = jnp.zeros(acc_ref.shape, F32)
        pp_ref[...] = jnp.zeros(pp_ref.shape, BF16)
        pa_ref[...] = jnp.ones(pa_ref.shape, F32)
        if streaming:
            for g in range(N_KV):
                m0 = seed_max(g)
                r0 = m0 - bias_cap(g, SEL_BLOCK - 1.0)
                m_ref[g] = r0
                mx_ref[g] = m0
                sh_ref[g] = r0 + bias_cap(g, SEL_CHUNK - 1.0)
            ex_ref[...] = jnp.zeros(ex_ref.shape, F32)
        else:
            m_ref[...] = jnp.full(m_ref.shape, NEG, F32)

        def online(g, sc, k0):
            m_cur = jnp.max(sc, axis=0, keepdims=True)
            m_prev = m_ref[g]
            if not streaming:
                m_new = jnp.maximum(m_prev, m_cur)
                m_ref[g] = m_new
                return jnp.exp((sc - m_new).astype(BF16)), jnp.exp(m_prev - m_new)
            cap = bias_cap(g, (k0 + (SEL_CHUNK - 1)).astype(F32))
            shift = m_prev + cap
            r_cur = m_cur - cap
            m_ref[g] = jnp.maximum(m_prev, r_cur)
            mx_ref[g] = jnp.maximum(mx_ref[g], m_cur)
            ex_ref[g] = jnp.maximum(ex_ref[g], r_cur - m_prev)
            alpha = jnp.exp(sh_ref[g] - shift)
            sh_ref[g] = shift
            return jnp.exp((sc - shift).astype(BF16)), alpha

        def step(c, src_ref, dst_ref):
            last = dst_ref is None
            k0 = pl.multiple_of(c * SEL_CHUNK, SEL_CHUNK)
            kprev = pl.multiple_of(jnp.maximum(c - 1, 0) * SEL_CHUNK, SEL_CHUNK)
            s1 = sel_scores(1, k0)
            pending_pv(kprev)
            if not last:
                dst_ref[...] = sel_scores(0, pl.multiple_of(k0 + SEL_CHUNK, SEL_CHUNK))
            s0 = src_ref[...]
            if last:
                cb = cbm_ref[(q0 // QB) % (SEL_CHUNK // QB)]
                s0 = add_heads(s0, cb)
                s1 = add_heads(s1, cb)
            p0, alpha0 = online(0, s0, k0)
            acc_ref[0] = alpha0 * acc_ref[0] + _dot(vst_ref[0, :, pl.ds(k0, SEL_CHUNK)], p0)
            p1, alpha1 = online(1, s1, k0)
            pp_ref[...] = p1
            pa_ref[...] = alpha1

        def body(c, carry):
            for par in range(2):
                pl.when(c % 2 == par)(functools.partial(step, c, s0_refs[par], s0_refs[1 - par]))
            return carry

        s0_refs[0][...] = sel_scores(0, 0)
        lax.fori_loop(0, nfull, body, 0)
        for par in range(2):
            pl.when(nfull % 2 == par)(functools.partial(step, nfull, s0_refs[par], None))
        pending_pv(pl.multiple_of(nfull * SEL_CHUNK, SEL_CHUNK))

    run_selected(True)
    unsafe = jnp.maximum(jnp.maximum(ex_ref[...], wx_ref[...]) - SHIFT_SLACK,
                         sh_ref[...] - mx_ref[...] - RANGE_SLACK)

    @pl.when(jnp.max(unsafe) > 0.0)
    def _():
        for g in range(N_KV):
            window(g, False)
        run_selected(False)


    gt = gate_ref[...].T
    outs = []
    for g in range(N_KV):
        acc = acc_ref[g]
        o_s = acc * (1.0 / acc[HEAD_DIM:HEAD_DIM + 1, :])
        o_c, o_w = o_cs[g], ow_ref[g]
        for r in range(R):
            col = (g * R + r) * 3
            cols = slice(r * QB, (r + 1) * QB)
            o = (gt[col:col + 1] * o_c[:HEAD_DIM, cols] + gt[col + 1:col + 2] * o_s[:HEAD_DIM, cols]
                 + gt[col + 2:col + 3] * o_w[:HEAD_DIM, cols])
            outs.append(o)
    o_ref[...] = jnp.concatenate(outs, axis=0).T.astype(BF16)


def _nsa(qa, gates, kca, vct, ksa, vst, kwa, vwt, ov, ssl, slv, seedm, farm, cbm, B, S):
    NC = S // CMP_STRIDE
    nqb = S // Q_BLOCK
    M = GROUP * Q_BLOCK
    tok = lambda w_: pl.BlockSpec((Q_BLOCK, w_), lambda b, i: (b * nqb + i, 0))
    const = lambda shape: pl.BlockSpec(shape, lambda b, i: (0,) * len(shape))
    seq = pl.BlockSpec((N_KV, S, LANES), lambda b, i: (0, b, 0))
    seq_t = pl.BlockSpec((N_KV, LANES, S), lambda b, i: (0, 0, b))
    return pl.pallas_call(
        functools.partial(_nsa_kernel, S=S),
        grid=(B, nqb),
        in_specs=[tok(N_HEADS * LANES), tok(LANES),
                  pl.BlockSpec((1, N_KV, NC, LANES), lambda b, i: (b, 0, 0, 0)),
                  pl.BlockSpec((1, N_KV, LANES, NC), lambda b, i: (b, 0, 0, 0)),
                  seq, seq_t, seq, seq_t,
                  const((SEL_BLOCK, NC)), const((N_HEADS, LANES)), const((N_KV, 1, M)),
                  const(seedm.shape), const(farm.shape), const(cbm.shape)],
        out_specs=tok(ATTN_W),
        out_shape=jax.ShapeDtypeStruct((B * S, ATTN_W), BF16),
        scratch_shapes=[pltpu.VMEM((N_KV, M, LANES), BF16), pltpu.VMEM((N_KV, 1, M), F32),
                        pltpu.VMEM((N_KV, LANES, M), F32),
                        pltpu.VMEM((SEL_CHUNK, M), BF16), pltpu.VMEM((1, M), F32),
                        pltpu.VMEM((SEL_CHUNK, M), F32), pltpu.VMEM((SEL_CHUNK, M), F32)]
                       + [pltpu.VMEM((N_KV, 1, M), F32)] * 4 + [pltpu.VMEM((N_KV, LANES, M), F32)],
        compiler_params=_params(("parallel", "arbitrary")),
    )(qa, gates, kca, vct, ksa, vst, kwa, vwt, ov, ssl, slv, seedm, farm, cbm)


def _mix_kernel(x_ref, attn_ref, u_ref, uh_ref, g1_ref, wm_ref, wap_ref, wpool_ref, ps_ref,
                wpp_ref, wout_ref, o_ref, *, nst):
    TM = x_ref.shape[0]
    seq_tile = pl.program_id(0) % nst
    x = x_ref[...]
    h = _rmsnorm(x, g1_ref[...]).astype(BF16)
    merge = jax.nn.sigmoid(_dot(h, wm_ref[...]))
    a = _dot(attn_ref[...], wap_ref[...])

    halo = jnp.where(seq_tile == 0, 0.0, uh_ref[...])
    ue = jnp.concatenate([halo, u_ref[...]], axis=0)
    pos = seq_tile * TM + lax.broadcasted_iota(jnp.int32, (TM, 1), 0)
    ys = []
    for gi, w in enumerate(POOL_WINDOWS):
        ug = ue[:, gi * POOL_DIM:(gi + 1) * POOL_DIM]
        s = ug
        sh = 1
        while sh < w:
            s = s + pltpu.roll(s, sh, 0)
            sh *= 2
        cnt = jnp.minimum(pos + 1, w).astype(F32)
        p = s[POOL_HALO:] / cnt - ug[POOL_HALO:]
        ys.append(_dot(p.astype(BF16), wpool_ref[gi]))
    y = jnp.concatenate(ys, axis=1) * ps_ref[...]
    b = _dot(y.astype(BF16), wpp_ref[...])
    mix = merge[:, :D_MODEL] * a + merge[:, D_MODEL:] * b
    o_ref[...] = x + _dot(mix.astype(BF16), wout_ref[...])


def _mix(x2, attn, u, g1, wm, wap, wpool, ps, wpp, wout, S):
    T = x2.shape[0]
    TM = TOK_TILE
    nst = S // TM
    tok = lambda w_: pl.BlockSpec((TM, w_), lambda i: (i, 0))
    const = _const_spec
    halo = pl.BlockSpec((POOL_HALO, POOL_W), lambda i: (jnp.maximum(i * (TM // POOL_HALO) - 1, 0), 0))
    return pl.pallas_call(
        functools.partial(_mix_kernel, nst=nst),
        grid=(T // TM,),
        in_specs=[tok(D_MODEL), tok(ATTN_W), tok(POOL_W), halo, const((1, D_MODEL)),
                  const((D_MODEL, 2 * D_MODEL)), const((ATTN_W, D_MODEL)),
                  const((POOL_GROUPS, POOL_DIM, POOL_DIM)), const((1, POOL_W)),
                  const((POOL_W, D_MODEL)), const((D_MODEL, D_MODEL))],
        out_specs=tok(D_MODEL),
        out_shape=jax.ShapeDtypeStruct((T, D_MODEL), F32),
        compiler_params=_params(("parallel",)),
    )(x2, attn, u, u, g1, wm, wap, wpool, ps, wpp, wout)


def _ffn_kernel(x_ref, g2_ref, w1_ref, w2_ref, gf_ref, o_ref, *, final):
    x = x_ref[...]
    h = _rmsnorm(x, g2_ref[...]).astype(BF16)
    acc = x
    for c in range(D_FF // FF_CHUNK):
        cols = slice(c * FF_CHUNK, (c + 1) * FF_CHUNK)
        t = jnp.square(jnp.maximum(_dot(h, w1_ref[:, cols]), 0.0)).astype(BF16)
        acc = acc + _dot(t, w2_ref[cols, :])
    if final:
        acc = _rmsnorm(acc, gf_ref[...])
    o_ref[...] = acc


def _ffn(x2, g2, w1, w2, gf, final):
    T = x2.shape[0]
    TM = TOK_TILE
    tok = pl.BlockSpec((TM, D_MODEL), lambda i: (i, 0))
    const = _const_spec
    return pl.pallas_call(
        functools.partial(_ffn_kernel, final=final),
        grid=(T // TM,),
        in_specs=[tok, const((1, D_MODEL)), const((D_MODEL, D_FF)), const((D_FF, D_MODEL)),
                  const((1, D_MODEL))],
        out_specs=tok,
        out_shape=jax.ShapeDtypeStruct((T, D_MODEL), F32),
        compiler_params=_params(("parallel",)),
    )(x2, g2, w1, w2, gf)


def _pack_w_in(w_in_l):
    scale = HEAD_DIM ** -0.5
    wq = w_in_l[:, :ATTN_W] * scale
    o_kv = ATTN_W
    o_gate = o_kv + 3 * KV_W
    o_pool = o_gate + 3 * N_HEADS
    o_merge = o_pool + POOL_W
    wgate = jnp.pad(w_in_l[:, o_gate:o_pool], ((0, 0), (0, LANES - 3 * N_HEADS)))
    w = jnp.concatenate([wq, w_in_l[:, o_kv:o_gate], wgate, w_in_l[:, o_pool:o_merge]], axis=1)
    return w.astype(BF16), w_in_l[:, o_merge:].astype(BF16)


def _pack_compress(pe_l, w1_l, w2_l):
    eye = jnp.eye(N_KV, dtype=F32)
    half = CMP_BLOCK // 2
    w1r = w1_l.reshape(2, 2, half, HEAD_DIM, CMP_HIDDEN)
    w1big = jnp.einsum('khldm,gG->hlkgdGm', w1r, eye)
    w1big = w1big.reshape(2, half, 2, N_KV * HEAD_DIM, N_KV * CMP_HIDDEN)
    per = pe_l.reshape(2, 2, half, HEAD_DIM)
    pebig = jnp.broadcast_to(per.transpose(1, 2, 0, 3)[:, :, :, None, :],
                             (2, half, 2, N_KV, HEAD_DIM)).reshape(2, half, 2, N_KV * HEAD_DIM)
    w2p = jnp.pad(w2_l, ((0, 0), (0, 0), (0, LANES - HEAD_DIM)))
    w2big = jnp.einsum('kme,kK,gG->kgmKGe', w2p, eye, eye)
    w2big = w2big.reshape(2 * N_KV * CMP_HIDDEN, 2 * N_KV * LANES)
    return pebig, w1big.astype(BF16), w2big.astype(BF16)


def _constants(S):
    NC = S // CMP_STRIDE
    n_sel = S // SEL_BLOCK
    pos = np.arange(S)
    ksc = np.zeros((S, LANES), np.float32)
    ksc[:, HEAD_DIM] = pos % SEL_BLOCK
    blk = pos // SEL_BLOCK
    sel_rows = blk >= 1
    ksc[pos[sel_rows], HEAD_DIM + blk[sel_rows]] = 1.0
    kwc = np.zeros((S, LANES), np.float32)
    kwc[:, HEAD_DIM] = pos % SEL_BLOCK
    kwc[:, HEAD_DIM + 1] = blk
    vc = np.zeros((1, LANES), np.float32)
    vc[0, HEAD_DIM] = 1.0
    qc = np.zeros((1, N_HEADS * LANES), np.float32)
    ssl = np.zeros((N_HEADS, LANES), np.float32)
    for h in range(N_HEADS):
        qc[0, h * LANES + HEAD_DIM] = SLOPES[h]
        qc[0, h * LANES + HEAD_DIM + 1] = SLOPES[h] * SEL_BLOCK
        ssl[h, HEAD_DIM + 2:] = SLOPES[h] * SEL_BLOCK * np.arange(2, SEL_BLOCK)
    c = np.arange(NC)
    cc = np.zeros((NC, LANES), np.float32)
    cstart = c * CMP_STRIDE
    cc[:, HEAD_DIM] = cstart % SEL_BLOCK
    cc[:, HEAD_DIM + 1] = cstart // SEL_BLOCK
    slv = np.repeat(np.asarray(SLOPES, np.float32).reshape(N_KV, 1, GROUP), Q_BLOCK, axis=2)
    row = np.arange(SEL_CHUNK)[:, None]
    col = np.arange(Q_BLOCK)[None, :]
    neg = lambda vis: np.where(vis, 0.0, NEG).astype(np.float32)
    seedm = neg(row[:Q_BLOCK] <= col)
    farm = []
    for i in range(WINDOW // Q_BLOCK + 1):
        first = max(i * Q_BLOCK - WINDOW, 0)
        kpos, tq = first + row[:WINDOW], i * Q_BLOCK + col
        farm.append(neg((kpos < i * Q_BLOCK) & (tq - kpos < WINDOW)))
    cbm = [neg(row <= off + col) for off in range(0, SEL_CHUNK, Q_BLOCK)]
    j = np.arange(SEL_BLOCK)
    ovt = ((cstart[None, :] <= j[:, None] * SEL_BLOCK + SEL_BLOCK - 1)
           & (cstart[None, :] + CMP_BLOCK - 1 >= j[:, None] * SEL_BLOCK)
           & (j[:, None] < n_sel) & (c[None, :] < NC - 1)).astype(np.float32)
    return (jnp.asarray(qc), jnp.asarray(ksc), jnp.asarray(kwc), jnp.asarray(vc), jnp.asarray(cc),
            jnp.asarray(ovt, BF16), jnp.asarray(ssl), jnp.asarray(slv),
            jnp.asarray(seedm), jnp.asarray(np.stack(farm)), jnp.asarray(np.stack(cbm)))


def kernel(x, norm1_g, w_in, cmp_pe, cmp_w1, cmp_w2, w_attn_proj, w_pool, pool_scale,
           w_pool_proj, w_out, norm2_g, w_ff1, w_ff2, final_g):
    B, S, _ = x.shape
    depth = w_in.shape[0]
    assert S % SEL_CHUNK == 0 and S % TOK_TILE == 0 and S // SEL_BLOCK <= SEL_BLOCK and S >= WINDOW + Q_BLOCK
    qc, ksc, kwc, vc, cc, ovt, ssl, slv, seedm, farm, cbm = _constants(S)
    x2 = x.reshape(B * S, D_MODEL)
    for l in range(depth):
        w_pack, w_merge = _pack_w_in(w_in[l])
        pebig, w1big, w2big = _pack_compress(cmp_pe[l], cmp_w1[l], cmp_w2[l])
        g1 = norm1_g[l].reshape(1, D_MODEL)
        qa, kvc, ksa, vst, kwa, vwt, gates, u = _inproj(x2, g1, w_pack, qc, ksc, kwc, vc, S)
        kca, vct = _compress(kvc, pebig, w1big, w2big, cc, B)
        attn = _nsa(qa, gates, kca, vct, ksa, vst, kwa, vwt, ovt, ssl, slv, seedm, farm, cbm, B, S)
        x2 = _mix(x2, attn, u, g1, w_merge, w_attn_proj[l].astype(BF16), w_pool[l].astype(BF16),
                  pool_scale[l].reshape(1, POOL_W), w_pool_proj[l].astype(BF16),
                  w_out[l].astype(BF16), S)
        x2 = _ffn(x2, norm2_g[l].reshape(1, D_MODEL), w_ff1[l].astype(BF16), w_ff2[l].astype(BF16),
                  final_g.reshape(1, D_MODEL), final=(l == depth - 1))
    return x2.reshape(B, S, D_MODEL)
```

```python
import functools

import numpy as np
import jax
import jax.numpy as jnp
from jax import lax
from jax.experimental import pallas as pl
from jax.experimental.pallas import tpu as pltpu

F32 = jnp.float32
BF16 = jnp.bfloat16

D_MODEL = 1024
N_HEADS = 8
HEAD_DIM = 64
N_KV = 2
GROUP = N_HEADS // N_KV
ATTN_W = N_HEADS * HEAD_DIM
KV_W = 2 * N_KV * HEAD_DIM
CMP_BLOCK = 32
CMP_STRIDE = 16
CMP_HIDDEN = 64
SEL_BLOCK = 64
SEL_TOP_N = 16
WINDOW = 512
POOL_GROUPS = 4
POOL_DIM = 128
POOL_W = POOL_GROUPS * POOL_DIM
POOL_WINDOWS = (2, 4, 8, 16)
POOL_HALO = 16
D_FF = 4 * D_MODEL
EPS = 1e-6
NEG = -1e30
FORCE = 1e6
UNSEL = -1e9
SLOPES = tuple(2.0 ** (-8.0 * (h + 1) / N_HEADS) for h in range(N_HEADS))

LANES = 128
SUBLANES = 8
Q_BLOCK = 256
SEL_CHUNK = 512
TOK_TILE = 1024
FF_CHUNK = 1024
VMEM_LIMIT = 56 * 1024 * 1024
SHIFT_SLACK = 12.0
RANGE_SLACK = 12.0

_C_Q = 0
_C_KC = _C_Q + ATTN_W
_C_KS = _C_KC + KV_W
_C_VS = _C_KS + LANES
_C_KW = _C_VS + LANES
_C_VW = _C_KW + LANES
_C_GATE = _C_VW + LANES
_C_U = _C_GATE + LANES
_C_END = _C_U + POOL_W


def _dot(a, b):
    return jnp.dot(a, b, preferred_element_type=F32)


def _dot_nt(a, b):
    return lax.dot_general(a, b, (((1,), (1,)), ((), ())), preferred_element_type=F32)


def _rmsnorm(x, g):
    return x * lax.rsqrt(jnp.mean(x * x, axis=-1, keepdims=True) + EPS) * g


def _const_spec(shape):
    return pl.BlockSpec(shape, lambda i: (0,) * len(shape), pipeline_mode=pl.Buffered(1))


def _params(sem):
    return pltpu.CompilerParams(dimension_semantics=sem, vmem_limit_bytes=VMEM_LIMIT)


def _inproj_kernel(x_ref, g_ref, w_ref, qc_ref, ksc_ref, kwc_ref, vc_ref,
                   qa_ref, kvc_ref, ksa_ref, vst_ref, kwa_ref, vwt_ref, gate_ref, u_ref):
    x = x_ref[...]
    h = _rmsnorm(x, g_ref[...]).astype(BF16)

    r_all = _dot(h, w_ref[...])

    def proj(a, b):
        return r_all[:, a:b]

    kvc_ref[0] = proj(_C_KC, _C_KC + LANES)
    kvc_ref[1] = proj(_C_KC + LANES, _C_KS)
    low = lax.broadcasted_iota(jnp.int32, (x.shape[0], LANES), 1) < HEAD_DIM

    def split(r, c0, c1):
        return jnp.where(low, r, c0), jnp.where(low, pltpu.roll(r, HEAD_DIM, 1), c1)

    rq = proj(_C_Q, _C_KC)
    for j in range(N_HEADS // 2):
        lanes = [slice((2 * j + i) * LANES, (2 * j + i + 1) * LANES) for i in range(2)]
        q0, q1 = split(rq[:, j * LANES:(j + 1) * LANES], qc_ref[:, lanes[0]], qc_ref[:, lanes[1]])
        qa_ref[:, lanes[0]] = q0.astype(BF16)
        qa_ref[:, lanes[1]] = q1.astype(BF16)
    for k_ref, v_ref, kc_ref, c0 in ((ksa_ref, vst_ref, ksc_ref, _C_KS), (kwa_ref, vwt_ref, kwc_ref, _C_KW)):
        k0, k1 = split(proj(c0, c0 + LANES), kc_ref[...], kc_ref[...])
        k_ref[0] = k0.astype(BF16)
        k_ref[1] = k1.astype(BF16)
        v0, v1 = split(proj(c0 + LANES, c0 + 2 * LANES), vc_ref[...], vc_ref[...])
        v_ref[0] = v0.T.astype(BF16)
        v_ref[1] = v1.T.astype(BF16)
    gate_ref[...] = jax.nn.sigmoid(proj(_C_GATE, _C_U))
    u_ref[...] = proj(_C_U, _C_END)


def _inproj(x2, g1, w, qc, ksc, kwc, vc, S):
    T = x2.shape[0]
    TM = TOK_TILE
    nst = S // TM
    tok = lambda w_: pl.BlockSpec((TM, w_), lambda i: (i, 0))
    const = _const_spec
    seq = pl.BlockSpec((TM, LANES), lambda i: (i % nst, 0))
    grp = pl.BlockSpec((N_KV, TM, LANES), lambda i: (0, i, 0))
    grp_t = pl.BlockSpec((N_KV, LANES, TM), lambda i: (0, 0, i))
    return pl.pallas_call(
        _inproj_kernel,
        grid=(T // TM,),
        in_specs=[tok(D_MODEL), const((1, D_MODEL)), const((D_MODEL, _C_END)),
                  const((1, N_HEADS * LANES)), seq, seq, const((1, LANES))],
        out_specs=[tok(N_HEADS * LANES), pl.BlockSpec((2, TM, LANES), lambda i: (0, i, 0)),
                   grp, grp_t, grp, grp_t, tok(LANES), tok(POOL_W)],
        out_shape=[jax.ShapeDtypeStruct((T, N_HEADS * LANES), BF16),
                   jax.ShapeDtypeStruct((2, T, LANES), F32),
                   jax.ShapeDtypeStruct((N_KV, T, LANES), BF16),
                   jax.ShapeDtypeStruct((N_KV, LANES, T), BF16),
                   jax.ShapeDtypeStruct((N_KV, T, LANES), BF16),
                   jax.ShapeDtypeStruct((N_KV, LANES, T), BF16),
                   jax.ShapeDtypeStruct((T, LANES), F32),
                   jax.ShapeDtypeStruct((T, POOL_W), F32)],
        compiler_params=_params(("parallel",)),
    )(x2, g1, w, qc, ksc, kwc, vc)


def _compress_kernel(x_ref, pe_ref, w1_ref, w2_ref, cc_ref, kca_ref, vct_ref):
    nc = x_ref.shape[1] // CMP_STRIDE
    hs = []
    for kv in range(2):
        a = jnp.zeros((nc, LANES), F32)
        b = jnp.zeros((nc, LANES), F32)
        for l in range(CMP_STRIDE):
            xl = x_ref[kv, pl.ds(l, nc, stride=CMP_STRIDE), :]
            a = a + _dot((xl + pe_ref[0, l, kv:kv + 1, :]).astype(BF16), w1_ref[0, l, kv])
            b = b + _dot((xl + pe_ref[1, l, kv:kv + 1, :]).astype(BF16), w1_ref[1, l, kv])
        hs.append(a + pltpu.roll(b, nc - 1, 0))
    h = jnp.concatenate(hs, axis=1)
    h = jax.nn.gelu(h).astype(BF16)
    out = _dot(h, w2_ref[...])
    cc = cc_ref[...]
    for g in range(N_KV):
        kca_ref[0, g] = (out[:, g * LANES:(g + 1) * LANES] + cc).astype(BF16)
        vct_ref[0, g] = out[:, (N_KV + g) * LANES:(N_KV + g + 1) * LANES].T.astype(BF16)


def _compress(kvc3, pe, w1, w2, cc, B):
    S = kvc3.shape[1] // B
    NC = S // CMP_STRIDE
    const = lambda shape: pl.BlockSpec(shape, lambda b: (0,) * len(shape))
    return pl.pallas_call(
        _compress_kernel,
        grid=(B,),
        in_specs=[pl.BlockSpec((2, S, LANES), lambda b: (0, b, 0)), const((2, CMP_STRIDE, 2, LANES)),
                  const((2, CMP_STRIDE, 2, LANES, LANES)),
                  const((2 * N_KV * CMP_HIDDEN, 2 * N_KV * LANES)), const((NC, LANES))],
        out_specs=[pl.BlockSpec((1, N_KV, NC, LANES), lambda b: (b, 0, 0, 0)),
                   pl.BlockSpec((1, N_KV, LANES, NC), lambda b: (b, 0, 0, 0))],
        out_shape=[jax.ShapeDtypeStruct((B, N_KV, NC, LANES), BF16),
                   jax.ShapeDtypeStruct((B, N_KV, LANES, NC), BF16)],
        compiler_params=_params(("parallel",)),
    )(kvc3, pe, w1, w2, cc)


def _nsa_kernel(qa_ref, gate_ref, kca_ref, vct_ref, ksa_ref, vst_ref, kwa_ref, vwt_ref,
                ov_ref, ssl_ref, slv_ref, seedm_ref, farm_ref, cbm_ref, o_ref, qsel_ref, m_ref, acc_ref, s0a_ref, s0b_ref,
                mx_ref, sh_ref, ex_ref, wx_ref, ow_ref, *, S):
    QB, R, M = Q_BLOCK, GROUP, GROUP * Q_BLOCK
    NC = S // CMP_STRIDE
    n_sel = S // SEL_BLOCK
    top_n = min(SEL_TOP_N, n_sel)
    WK = WINDOW + QB
    q0 = pl.program_id(1) * QB
    s0_refs = (s0a_ref, s0b_ref)

    def heads(a):
        return jnp.concatenate([a] * R, axis=1)

    def add_heads(x, a):
        return jnp.concatenate([x[:, r * QB:(r + 1) * QB] + a for r in range(R)], axis=1)

    ce = lax.broadcasted_iota(jnp.int32, (NC, QB), 0) * CMP_STRIDE + (CMP_BLOCK - 1)
    vis = (q0 + lax.broadcasted_iota(jnp.int32, (NC, QB), 1)) >= ce
    vbias = jnp.where(vis, 0.0, NEG)
    any_vis = heads((q0 + lax.broadcasted_iota(jnp.int32, (1, QB), 1)) >= CMP_BLOCK - 1)

    jrow = lax.broadcasted_iota(jnp.int32, (SEL_BLOCK, QB), 0)
    cur = (q0 + lax.broadcasted_iota(jnp.int32, (SEL_BLOCK, QB), 1)) // SEL_BLOCK
    future = jrow > cur
    forced = (jrow == 0) | (jrow == cur) | (jrow == cur - 1)
    sub8 = lax.broadcasted_iota(jnp.int32, (SUBLANES, QB), 0)

    ws = pl.multiple_of(jnp.maximum(q0 - WINDOW, 0), QB)

    qas = [jnp.concatenate([qa_ref[:, (g * R + r) * LANES:(g * R + r + 1) * LANES] for r in range(R)],
                           axis=0) for g in range(N_KV)]
    s_cs = [_dot_nt(kca_ref[0, g], qas[g]) for g in range(N_KV)]

    o_cs, imps = [], []
    for g in range(N_KV):
        es, invs, psum = [], [], 0.0
        for r in range(R):
            sr = s_cs[g][:, r * QB:(r + 1) * QB] + vbias
            er = jnp.exp(sr - jnp.max(sr, axis=0, keepdims=True))
            ir = jnp.where(any_vis[:, :QB], 1.0 / jnp.sum(er, axis=0, keepdims=True), 0.0)
            psum = psum + er * ir
            es.append(er.astype(BF16))
            invs.append(ir)
        o_cs.append(_dot(vct_ref[0, g], jnp.concatenate(es, axis=1)) * jnp.concatenate(invs, axis=1))
        hi = psum.astype(BF16)
        r1 = psum - hi.astype(F32)
        mid = r1.astype(BF16)
        lo = (r1 - mid.astype(F32)).astype(BF16)
        ov = ov_ref[...]
        imps.append(_dot(ov, hi) + _dot(ov, mid) + _dot(ov, lo))

    def select(g):
        v = jnp.where(future, NEG, jnp.where(forced, FORCE, imps[g]))
        vt = [v[k * SUBLANES:(k + 1) * SUBLANES] for k in range(SEL_BLOCK // SUBLANES)]
        cnt = [jnp.zeros((SUBLANES, QB), jnp.int32) for _ in vt]
        for i in range(n_sel):
            vi = v[i:i + 1, :]
            for k, vk in enumerate(vt):
                if (k + 1) * SUBLANES <= i + 1:
                    beats = jnp.where(vi > vk, 1, 0)
                elif k * SUBLANES > i:
                    beats = jnp.where(vi >= vk, 1, 0)
                else:
                    beats = jnp.where(sub8 > i - k * SUBLANES, jnp.where(vi >= vk, 1, 0),
                                      jnp.where(vi > vk, 1, 0))
                cnt[k] = cnt[k] + beats
        cnt = jnp.concatenate(cnt, axis=0)
        selb = jnp.where(cnt < top_n, jnp.where(v > NEG / 2, 0.0, UNSEL), UNSEL)
        aug = jnp.concatenate([jnp.zeros_like(selb), selb], axis=0).T
        for r in range(R):
            qh = qas[g][r * QB:(r + 1) * QB].astype(F32)
            qsel_ref[g, r * QB:(r + 1) * QB, :] = (
                qh + aug + ssl_ref[g * R + r:g * R + r + 1, :]).astype(BF16)

    def window(g, streaming):
        def positions(n, k0):
            return (q0 + lax.broadcasted_iota(jnp.int32, (n, QB), 1),
                    k0 + lax.broadcasted_iota(jnp.int32, (n, QB), 0))

        if not streaming:
            tq, kpos = positions(WK, ws)
            dw = tq - kpos
            sw = add_heads(_dot_nt(kwa_ref[g, pl.ds(ws, WK), :], qas[g]),
                           jnp.where(dw >= 0, jnp.where(dw < WINDOW, 0.0, NEG), NEG))
            pw = jnp.exp((sw - jnp.max(sw, axis=0, keepdims=True)).astype(BF16))
            accw = _dot(vwt_ref[g, :, pl.ds(ws, WK)], pw)
        else:
            qs = pl.multiple_of(q0, QB)
            sd = add_heads(_dot_nt(kwa_ref[g, pl.ds(qs, QB), :], qas[g]), seedm_ref[...])
            md = jnp.max(sd, axis=0, keepdims=True)
            pd = jnp.exp((sd - md).astype(BF16))
            far = farm_ref[jnp.minimum(q0 // QB, WINDOW // QB)]
            sf = add_heads(_dot_nt(kwa_ref[g, pl.ds(ws, WINDOW), :], qas[g]), far)
            wx_ref[g] = jnp.max(sf, axis=0, keepdims=True) - md
            pf = jnp.exp((sf - md).astype(BF16))
            accw = (_dot(vwt_ref[g, :, pl.ds(qs, QB)], pd) + _dot(vwt_ref[g, :, pl.ds(ws, WINDOW)], pf))
        ow_ref[g] = accw * (1.0 / accw[HEAD_DIM:HEAD_DIM + 1, :])

    def sel_scores(g, k0):
        return _dot_nt(ksa_ref[g, pl.ds(k0, SEL_CHUNK), :], qsel_ref[g])

    def seed_max(g):
        kpos0 = lax.broadcasted_iota(jnp.int32, (SEL_BLOCK, QB), 0)
        tq0 = q0 + lax.broadcasted_iota(jnp.int32, (SEL_BLOCK, QB), 1)
        t0 = add_heads(_dot_nt(ksa_ref[g, 0:SEL_BLOCK, :], qsel_ref[g]), jnp.where(kpos0 <= tq0, 0.0, NEG))
        return jnp.max(t0, axis=0, keepdims=True)

    select(0)
    window(0, True)
    select(1)
    window(1, True)

    tvec = heads(q0 + lax.broadcasted_iota(jnp.int32, (1, QB), 1)).astype(F32)

    def bias_cap(g, k_hi):
        return slv_ref[g] * jnp.minimum(tvec, k_hi)

    nfull = q0 // SEL_CHUNK

    def run_selected(streaming):
        acc_ref[...] = jnp.zeros(acc_ref.shape, F32)
        if streaming:
            for g in range(N_KV):
                m0 = seed_max(g)
                r0 = m0 - bias_cap(g, SEL_BLOCK - 1.0)
                m_ref[g] = r0
                mx_ref[g] = m0
                sh_ref[g] = r0 + bias_cap(g, SEL_CHUNK - 1.0)
            ex_ref[...] = jnp.zeros(ex_ref.shape, F32)
        else:
            m_ref[...] = jnp.full(m_ref.shape, NEG, F32)

        def online(g, sc, k0):
            m_cur = jnp.max(sc, axis=0, keepdims=True)
            m_prev = m_ref[g]
            if not streaming:
                m_new = jnp.maximum(m_prev, m_cur)
                m_ref[g] = m_new
                return jnp.exp((sc - m_new).astype(BF16)), jnp.exp(m_prev - m_new)
            cap = bias_cap(g, (k0 + (SEL_CHUNK - 1)).astype(F32))
            shift = m_prev + cap
            r_cur = m_cur - cap
            m_ref[g] = jnp.maximum(m_prev, r_cur)
            mx_ref[g] = jnp.maximum(mx_ref[g], m_cur)
            ex_ref[g] = jnp.maximum(ex_ref[g], r_cur - m_prev)
            alpha = jnp.exp(sh_ref[g] - shift)
            sh_ref[g] = shift
            return jnp.exp((sc - shift).astype(BF16)), alpha

        def step(c, src_ref, dst_ref):
            last = dst_ref is None
            k0 = pl.multiple_of(c * SEL_CHUNK, SEL_CHUNK)
            s1 = sel_scores(1, k0)
            if not last:
                dst_ref[...] = sel_scores(0, pl.multiple_of(k0 + SEL_CHUNK, SEL_CHUNK))
            s0 = src_ref[...]
            if last:
                cb = cbm_ref[(q0 // QB) % (SEL_CHUNK // QB)]
                s0 = add_heads(s0, cb)
                s1 = add_heads(s1, cb)
            p0, alpha0 = online(0, s0, k0)
            acc_ref[0] = alpha0 * acc_ref[0] + _dot(vst_ref[0, :, pl.ds(k0, SEL_CHUNK)], p0)
            p1, alpha1 = online(1, s1, k0)
            acc_ref[1] = alpha1 * acc_ref[1] + _dot(vst_ref[1, :, pl.ds(k0, SEL_CHUNK)], p1)

        def body(c, carry):
            for par in range(2):
                pl.when(c % 2 == par)(functools.partial(step, c, s0_refs[par], s0_refs[1 - par]))
            return carry

        s0_refs[0][...] = sel_scores(0, 0)
        lax.fori_loop(0, nfull, body, 0)
        for par in range(2):
            pl.when(nfull % 2 == par)(functools.partial(step, nfull, s0_refs[par], None))

    run_selected(True)
    unsafe = jnp.maximum(jnp.maximum(ex_ref[...], wx_ref[...]) - SHIFT_SLACK,
                         sh_ref[...] - mx_ref[...] - RANGE_SLACK)

    @pl.when(jnp.max(unsafe) > 0.0)
    def _():
        for g in range(N_KV):
            window(g, False)
        run_selected(False)


    gt = gate_ref[...].T
    outs = []
    for g in range(N_KV):
        acc = acc_ref[g]
        o_s = acc * (1.0 / acc[HEAD_DIM:HEAD_DIM + 1, :])
        o_c, o_w = o_cs[g], ow_ref[g]
        for r in range(R):
            col = (g * R + r) * 3
            cols = slice(r * QB, (r + 1) * QB)
            o = (gt[col:col + 1] * o_c[:HEAD_DIM, cols] + gt[col + 1:col + 2] * o_s[:HEAD_DIM, cols]
                 + gt[col + 2:col + 3] * o_w[:HEAD_DIM, cols])
            outs.append(o)
    o_ref[...] = jnp.concatenate(outs, axis=0).T.astype(BF16)


def _nsa(qa, gates, kca, vct, ksa, vst, kwa, vwt, ov, ssl, slv, seedm, farm, cbm, B, S):
    NC = S // CMP_STRIDE
    nqb = S // Q_BLOCK
    M = GROUP * Q_BLOCK
    tok = lambda w_: pl.BlockSpec((Q_BLOCK, w_), lambda b, i: (b * nqb + i, 0))
    const = lambda shape: pl.BlockSpec(shape, lambda b, i: (0,) * len(shape))
    seq = pl.BlockSpec((N_KV, S, LANES), lambda b, i: (0, b, 0))
    seq_t = pl.BlockSpec((N_KV, LANES, S), lambda b, i: (0, 0, b))
    return pl.pallas_call(
        functools.partial(_nsa_kernel, S=S),
        grid=(B, nqb),
        in_specs=[tok(N_HEADS * LANES), tok(LANES),
                  pl.BlockSpec((1, N_KV, NC, LANES), lambda b, i: (b, 0, 0, 0)),
                  pl.BlockSpec((1, N_KV, LANES, NC), lambda b, i: (b, 0, 0, 0)),
                  seq, seq_t, seq, seq_t,
                  const((SEL_BLOCK, NC)), const((N_HEADS, LANES)), const((N_KV, 1, M)),
                  const(seedm.shape), const(farm.shape), const(cbm.shape)],
        out_specs=tok(ATTN_W),
        out_shape=jax.ShapeDtypeStruct((B * S, ATTN_W), BF16),
        scratch_shapes=[pltpu.VMEM((N_KV, M, LANES), BF16), pltpu.VMEM((N_KV, 1, M), F32),
                        pltpu.VMEM((N_KV, LANES, M), F32),
                        pltpu.VMEM((SEL_CHUNK, M), F32), pltpu.VMEM((SEL_CHUNK, M), F32)]
                       + [pltpu.VMEM((N_KV, 1, M), F32)] * 4 + [pltpu.VMEM((N_KV, LANES, M), F32)],
        compiler_params=_params(("parallel", "arbitrary")),
    )(qa, gates, kca, vct, ksa, vst, kwa, vwt, ov, ssl, slv, seedm, farm, cbm)


def _mix_kernel(x_ref, attn_ref, u_ref, uh_ref, g1_ref, wm_ref, wap_ref, wpool_ref, ps_ref,
                wpp_ref, wout_ref, o_ref, *, nst):
    TM = x_ref.shape[0]
    seq_tile = pl.program_id(0) % nst
    x = x_ref[...]
    h = _rmsnorm(x, g1_ref[...]).astype(BF16)
    merge = jax.nn.sigmoid(_dot(h, wm_ref[...]))
    a = _dot(attn_ref[...], wap_ref[...])

    halo = jnp.where(seq_tile == 0, 0.0, uh_ref[...])
    ue = jnp.concatenate([halo, u_ref[...]], axis=0)
    pos = seq_tile * TM + lax.broadcasted_iota(jnp.int32, (TM, 1), 0)
    ys = []
    for gi, w in enumerate(POOL_WINDOWS):
        ug = ue[:, gi * POOL_DIM:(gi + 1) * POOL_DIM]
        s = ug
        sh = 1
        while sh < w:
            s = s + pltpu.roll(s, sh, 0)
            sh *= 2
        cnt = jnp.minimum(pos + 1, w).astype(F32)
        p = s[POOL_HALO:] / cnt - ug[POOL_HALO:]
        ys.append(_dot(p.astype(BF16), wpool_ref[gi]))
    y = jnp.concatenate(ys, axis=1) * ps_ref[...]
    b = _dot(y.astype(BF16), wpp_ref[...])
    mix = merge[:, :D_MODEL] * a + merge[:, D_MODEL:] * b
    o_ref[...] = x + _dot(mix.astype(BF16), wout_ref[...])


def _mix(x2, attn, u, g1, wm, wap, wpool, ps, wpp, wout, S):
    T = x2.shape[0]
    TM = TOK_TILE
    nst = S // TM
    tok = lambda w_: pl.BlockSpec((TM, w_), lambda i: (i, 0))
    const = _const_spec
    halo = pl.BlockSpec((POOL_HALO, POOL_W), lambda i: (jnp.maximum(i * (TM // POOL_HALO) - 1, 0), 0))
    return pl.pallas_call(
        functools.partial(_mix_kernel, nst=nst),
        grid=(T // TM,),
        in_specs=[tok(D_MODEL), tok(ATTN_W), tok(POOL_W), halo, const((1, D_MODEL)),
                  const((D_MODEL, 2 * D_MODEL)), const((ATTN_W, D_MODEL)),
                  const((POOL_GROUPS, POOL_DIM, POOL_DIM)), const((1, POOL_W)),
                  const((POOL_W, D_MODEL)), const((D_MODEL, D_MODEL))],
        out_specs=tok(D_MODEL),
        out_shape=jax.ShapeDtypeStruct((T, D_MODEL), F32),
        compiler_params=_params(("parallel",)),
    )(x2, attn, u, u, g1, wm, wap, wpool, ps, wpp, wout)


def _ffn_kernel(x_ref, g2_ref, w1_ref, w2_ref, gf_ref, o_ref, *, final):
    x = x_ref[...]
    h = _rmsnorm(x, g2_ref[...]).astype(BF16)
    acc = x
    for c in range(D_FF // FF_CHUNK):
        cols = slice(c * FF_CHUNK, (c + 1) * FF_CHUNK)
        t = jnp.square(jnp.maximum(_dot(h, w1_ref[:, cols]), 0.0)).astype(BF16)
        acc = acc + _dot(t, w2_ref[cols, :])
    if final:
        acc = _rmsnorm(acc, gf_ref[...])
    o_ref[...] = acc


def _ffn(x2, g2, w1, w2, gf, final):
    T = x2.shape[0]
    TM = TOK_TILE
    tok = pl.BlockSpec((TM, D_MODEL), lambda i: (i, 0))
    const = _const_spec
    return pl.pallas_call(
        functools.partial(_ffn_kernel, final=final),
        grid=(T // TM,),
        in_specs=[tok, const((1, D_MODEL)), const((D_MODEL, D_FF)), const((D_FF, D_MODEL)),
                  const((1, D_MODEL))],
        out_specs=tok,
        out_shape=jax.ShapeDtypeStruct((T, D_MODEL), F32),
        compiler_params=_params(("parallel",)),
    )(x2, g2, w1, w2, gf)


def _pack_w_in(w_in_l):
    scale = HEAD_DIM ** -0.5
    wq = w_in_l[:, :ATTN_W] * scale
    o_kv = ATTN_W
    o_gate = o_kv + 3 * KV_W
    o_pool = o_gate + 3 * N_HEADS
    o_merge = o_pool + POOL_W
    wgate = jnp.pad(w_in_l[:, o_gate:o_pool], ((0, 0), (0, LANES - 3 * N_HEADS)))
    w = jnp.concatenate([wq, w_in_l[:, o_kv:o_gate], wgate, w_in_l[:, o_pool:o_merge]], axis=1)
    return w.astype(BF16), w_in_l[:, o_merge:].astype(BF16)


def _pack_compress(pe_l, w1_l, w2_l):
    eye = jnp.eye(N_KV, dtype=F32)
    half = CMP_BLOCK // 2
    w1r = w1_l.reshape(2, 2, half, HEAD_DIM, CMP_HIDDEN)
    w1big = jnp.einsum('khldm,gG->hlkgdGm', w1r, eye)
    w1big = w1big.reshape(2, half, 2, N_KV * HEAD_DIM, N_KV * CMP_HIDDEN)
    per = pe_l.reshape(2, 2, half, HEAD_DIM)
    pebig = jnp.broadcast_to(per.transpose(1, 2, 0, 3)[:, :, :, None, :],
                             (2, half, 2, N_KV, HEAD_DIM)).reshape(2, half, 2, N_KV * HEAD_DIM)
    w2p = jnp.pad(w2_l, ((0, 0), (0, 0), (0, LANES - HEAD_DIM)))
    w2big = jnp.einsum('kme,kK,gG->kgmKGe', w2p, eye, eye)
    w2big = w2big.reshape(2 * N_KV * CMP_HIDDEN, 2 * N_KV * LANES)
    return pebig, w1big.astype(BF16), w2big.astype(BF16)


def _constants(S):
    NC = S // CMP_STRIDE
    n_sel = S // SEL_BLOCK
    pos = np.arange(S)
    ksc = np.zeros((S, LANES), np.float32)
    ksc[:, HEAD_DIM] = pos % SEL_BLOCK
    blk = pos // SEL_BLOCK
    sel_rows = blk >= 1
    ksc[pos[sel_rows], HEAD_DIM + blk[sel_rows]] = 1.0
    kwc = np.zeros((S, LANES), np.float32)
    kwc[:, HEAD_DIM] = pos % SEL_BLOCK
    kwc[:, HEAD_DIM + 1] = blk
    vc = np.zeros((1, LANES), np.float32)
    vc[0, HEAD_DIM] = 1.0
    qc = np.zeros((1, N_HEADS * LANES), np.float32)
    ssl = np.zeros((N_HEADS, LANES), np.float32)
    for h in range(N_HEADS):
        qc[0, h * LANES + HEAD_DIM] = SLOPES[h]
        qc[0, h * LANES + HEAD_DIM + 1] = SLOPES[h] * SEL_BLOCK
        ssl[h, HEAD_DIM + 2:] = SLOPES[h] * SEL_BLOCK * np.arange(2, SEL_BLOCK)
    c = np.arange(NC)
    cc = np.zeros((NC, LANES), np.float32)
    cstart = c * CMP_STRIDE
    cc[:, HEAD_DIM] = cstart % SEL_BLOCK
    cc[:, HEAD_DIM + 1] = cstart // SEL_BLOCK
    slv = np.repeat(np.asarray(SLOPES, np.float32).reshape(N_KV, 1, GROUP), Q_BLOCK, axis=2)
    row = np.arange(SEL_CHUNK)[:, None]
    col = np.arange(Q_BLOCK)[None, :]
    neg = lambda vis: np.where(vis, 0.0, NEG).astype(np.float32)
    seedm = neg(row[:Q_BLOCK] <= col)
    farm = []
    for i in range(WINDOW // Q_BLOCK + 1):
        first = max(i * Q_BLOCK - WINDOW, 0)
        kpos, tq = first + row[:WINDOW], i * Q_BLOCK + col
        farm.append(neg((kpos < i * Q_BLOCK) & (tq - kpos < WINDOW)))
    cbm = [neg(row <= off + col) for off in range(0, SEL_CHUNK, Q_BLOCK)]
    j = np.arange(SEL_BLOCK)
    ovt = ((cstart[None, :] <= j[:, None] * SEL_BLOCK + SEL_BLOCK - 1)
           & (cstart[None, :] + CMP_BLOCK - 1 >= j[:, None] * SEL_BLOCK)
           & (j[:, None] < n_sel) & (c[None, :] < NC - 1)).astype(np.float32)
    return (jnp.asarray(qc), jnp.asarray(ksc), jnp.asarray(kwc), jnp.asarray(vc), jnp.asarray(cc),
            jnp.asarray(ovt, BF16), jnp.asarray(ssl), jnp.asarray(slv),
            jnp.asarray(seedm), jnp.asarray(np.stack(farm)), jnp.asarray(np.stack(cbm)))


def kernel(x, norm1_g, w_in, cmp_pe, cmp_w1, cmp_w2, w_attn_proj, w_pool, pool_scale,
           w_pool_proj, w_out, norm2_g, w_ff1, w_ff2, final_g):
    B, S, _ = x.shape
    depth = w_in.shape[0]
    assert S % SEL_CHUNK == 0 and S % TOK_TILE == 0 and S // SEL_BLOCK <= SEL_BLOCK and S >= WINDOW + Q_BLOCK
    qc, ksc, kwc, vc, cc, ovt, ssl, slv, seedm, farm, cbm = _constants(S)
    x2 = x.reshape(B * S, D_MODEL)
    for l in range(depth):
        w_pack, w_merge = _pack_w_in(w_in[l])
        pebig, w1big, w2big = _pack_compress(cmp_pe[l], cmp_w1[l], cmp_w2[l])
        g1 = norm1_g[l].reshape(1, D_MODEL)
        qa, kvc, ksa, vst, kwa, vwt, gates, u = _inproj(x2, g1, w_pack, qc, ksc, kwc, vc, S)
        kca, vct = _compress(kvc, pebig, w1big, w2big, cc, B)
        attn = _nsa(qa, gates, kca, vct, ksa, vst, kwa, vwt, ovt, ssl, slv, seedm, farm, cbm, B, S)
        x2 = _mix(x2, attn, u, g1, w_merge, w_attn_proj[l].astype(BF16), w_pool[l].astype(BF16),
                  pool_scale[l].reshape(1, POOL_W), w_pool_proj[l].astype(BF16),
                  w_out[l].astype(BF16), S)
        x2 = _ffn(x2, norm2_g[l].reshape(1, D_MODEL), w_ff1[l].astype(BF16), w_ff2[l].astype(BF16),
                  final_g.reshape(1, D_MODEL), final=(l == depth - 1))
    return x2.reshape(B, S, D_MODEL)
```
